```python
import jax, jax.numpy as jnp
from jax import lax
import numpy as np

D_MODEL = 2048
BATCH = 4
SEQ = 4096
DEPTH = 1

CHUNK = 64
N_META = 16
D_CONV = 2048
CONV_WIDTH = 31
D_RNN = 2560
RNN_HEADS = 16
RNN_HEAD_DIM = D_RNN // RNN_HEADS
RNN_CONV_WIDTH = 4
RG_C = 8.0
N_BRANCH = 2
N_EXPERTS = 32
TOP_K = 4
D_FF = 2048
SWIGLU_ALPHA = 1.702
SWIGLU_LIMIT = 7.0
MOE_BLOCK = 256
EPS = 1e-6
D_IN = 2 * D_CONV + 2 * D_RNN + N_BRANCH * D_MODEL

kernel_name = "hybrid_conformer_rglru_moe_block"


def rmsnorm(x, g):
    xf = x.astype(jnp.float32)
    y = xf * lax.rsqrt(jnp.mean(xf * xf, axis=-1, keepdims=True) + EPS)
    return (y * g.astype(jnp.float32)).astype(x.dtype)


def layernorm(x, g, b):
    xf = x.astype(jnp.float32)
    mu = jnp.mean(xf, axis=-1, keepdims=True)
    var = jnp.mean(jnp.square(xf - mu), axis=-1, keepdims=True)
    y = (xf - mu) * lax.rsqrt(var + EPS)
    return (y * g.astype(jnp.float32) + b.astype(jnp.float32)).astype(x.dtype)


def causal_depthwise_conv(x, w, b):
    width = w.shape[0]
    xp = jnp.pad(x, ((0, 0), (width - 1, 0), (0, 0)))
    y = lax.conv_general_dilated(xp, w[:, None, :], window_strides=(1,), padding='VALID',
                                 dimension_numbers=('NWC', 'WIO', 'NWC'),
                                 feature_group_count=x.shape[-1])
    return y + b


def conformer_conv_branch(u, dw_w, dw_b, ln_g, ln_b, w_pw, b_pw):
    a, gte = u[..., :D_CONV], u[..., D_CONV:]
    v = a * jax.nn.sigmoid(gte)
    v = causal_depthwise_conv(v, dw_w, dw_b)
    v = layernorm(v, ln_g, ln_b)
    v = jax.nn.silu(v)
    return v @ w_pw + b_pw


def _lin_combine(c1, c2):
    a1, b1 = c1
    a2, b2 = c2
    return a1 * a2, a2 * b1 + b2


def rglru_branch(xr, gr, cw, cb, wa, ba, wx, bx, lam, w_o, b_o):
    Bn, L, _ = xr.shape
    xr = causal_depthwise_conv(xr, cw, cb)
    xh = xr.reshape(Bn, L, RNN_HEADS, RNN_HEAD_DIM)
    r = jax.nn.sigmoid(jnp.einsum('blhi,hij->blhj', xh, wa) + ba).reshape(Bn, L, D_RNN)
    i = jax.nn.sigmoid(jnp.einsum('blhi,hij->blhj', xh, wx) + bx).reshape(Bn, L, D_RNN)
    log_a = -RG_C * r.astype(jnp.float32) * jax.nn.softplus(-lam.astype(jnp.float32))
    a = jnp.exp(log_a)
    mult = jnp.sqrt(jnp.maximum(1.0 - jnp.exp(2.0 * log_a), 0.0))
    bterm = mult * (i * xr).astype(jnp.float32)
    _, h = lax.associative_scan(_lin_combine, (a, bterm), axis=1)
    y = h.astype(xr.dtype) * jax.nn.gelu(gr)
    return y @ w_o + b_o


def moe_ffn(xn, w_router, b_router, w_gu, b_gu, w_dn, b_dn):
    Bn, L, D = xn.shape
    T = Bn * L
    xt = xn.reshape(T, D)
    logits = (xt @ w_router + b_router).astype(jnp.float32)
    top_val, top_idx = lax.top_k(logits, TOP_K)
    gates = jax.nn.softmax(top_val, axis=-1)
    A = T * TOP_K
    expert_flat = top_idx.reshape(A)
    token_flat = jnp.arange(A, dtype=jnp.int32) // TOP_K
    order = jnp.argsort(expert_flat)
    e_sorted = expert_flat[order]
    counts = jnp.zeros((N_EXPERTS,), jnp.int32).at[expert_flat].add(1)
    start = jnp.cumsum(counts) - counts
    padded = (counts + MOE_BLOCK - 1) // MOE_BLOCK * MOE_BLOCK
    pad_end = jnp.cumsum(padded)
    pad_start = pad_end - padded
    rank = jnp.arange(A, dtype=jnp.int32) - start[e_sorted]
    dest_sorted = pad_start[e_sorted] + rank
    n_blocks = -(-(A + N_EXPERTS * (MOE_BLOCK - 1)) // MOE_BLOCK)
    P = n_blocks * MOE_BLOCK
    dest = jnp.zeros((A,), jnp.int32).at[order].set(dest_sorted)
    tok_buf = jnp.full((P,), T, jnp.int32).at[dest_sorted].set(token_flat[order])
    x_pad = jnp.concatenate([xt, jnp.zeros((1, D), xt.dtype)], axis=0)
    x_buf = x_pad[tok_buf].reshape(n_blocks, MOE_BLOCK, D)
    block_expert = jnp.minimum(
        jnp.searchsorted(pad_end, jnp.arange(n_blocks, dtype=jnp.int32) * MOE_BLOCK, side='right'),
        N_EXPERTS - 1)

    def expert_block(args):
        xb, e = args
        gu = xb @ w_gu[e] + b_gu[e]
        g, u = gu[:, :D_FF], gu[:, D_FF:]
        g = jnp.minimum(g, SWIGLU_LIMIT)
        u = jnp.clip(u, -SWIGLU_LIMIT, SWIGLU_LIMIT)
        act = g * jax.nn.sigmoid(SWIGLU_ALPHA * g) * (u + 1.0)
        return act @ w_dn[e] + b_dn[e]

    y_buf = lax.map(expert_block, (x_buf, block_expert)).reshape(P, D)
    y_assign = y_buf[dest].reshape(T, TOP_K, D)
    y = jnp.einsum('tk,tkd->td', gates.astype(y_assign.dtype), y_assign)
    return y.reshape(Bn, L, D)


def setup_inputs(seed: int = 0) -> dict:
    key = jax.random.key(seed)
    ks = jax.random.split(key, 32)
    f32 = jnp.float32
    nrm = lambda k, shape, s: jax.random.normal(k, shape, f32) * s
    a0 = jax.random.uniform(ks[17], (DEPTH, D_RNN), f32, minval=0.9, maxval=0.999)
    s0 = a0 ** (1.0 / RG_C)
    rg_lambda = jnp.log(s0) - jnp.log1p(-s0)
    return {
        "x": nrm(ks[0], (BATCH, SEQ, D_MODEL), 1.0),
        "meta_tokens": nrm(ks[1], (N_META, D_MODEL), 1.0),
        "norm_mix_g": 1.0 + nrm(ks[2], (DEPTH, D_MODEL), 0.02),
        "w_in": nrm(ks[3], (DEPTH, D_MODEL, D_IN), D_MODEL ** -0.5),
        "b_in": nrm(ks[4], (DEPTH, D_IN), 0.01),
        "conv_dw_w": nrm(ks[5], (DEPTH, CONV_WIDTH, D_CONV), CONV_WIDTH ** -0.5),
        "conv_dw_b": nrm(ks[6], (DEPTH, D_CONV), 0.01),
        "conv_ln_g": 1.0 + nrm(ks[7], (DEPTH, D_CONV), 0.02),
        "conv_ln_b": nrm(ks[8], (DEPTH, D_CONV), 0.01),
        "w_conv_out": nrm(ks[9], (DEPTH, D_CONV, D_MODEL), D_CONV ** -0.5),
        "b_conv_out": nrm(ks[10], (DEPTH, D_MODEL), 0.01),
        "rnn_conv_w": nrm(ks[11], (DEPTH, RNN_CONV_WIDTH, D_RNN), RNN_CONV_WIDTH ** -0.5),
        "rnn_conv_b": nrm(ks[12], (DEPTH, D_RNN), 0.01),
        "rg_a_w": nrm(ks[13], (DEPTH, RNN_HEADS, RNN_HEAD_DIM, RNN_HEAD_DIM), RNN_HEAD_DIM ** -0.5),
        "rg_a_b": nrm(ks[14], (DEPTH, RNN_HEADS, RNN_HEAD_DIM), 0.01),
        "rg_x_w": nrm(ks[15], (DEPTH, RNN_HEADS, RNN_HEAD_DIM, RNN_HEAD_DIM), RNN_HEAD_DIM ** -0.5),
        "rg_x_b": nrm(ks[16], (DEPTH, RNN_HEADS, RNN_HEAD_DIM), 0.01),
        "rg_lambda": rg_lambda,
        "w_rnn_out": nrm(ks[18], (DEPTH, D_RNN, D_MODEL), D_RNN ** -0.5),
        "b_rnn_out": nrm(ks[19], (DEPTH, D_MODEL), 0.01),
        "w_out": nrm(ks[20], (DEPTH, D_MODEL, D_MODEL), D_MODEL ** -0.5),
        "b_out": nrm(ks[21], (DEPTH, D_MODEL), 0.01),
        "norm_ffn_g": 1.0 + nrm(ks[22], (DEPTH, D_MODEL), 0.02),
        "w_router": nrm(ks[23], (DEPTH, D_MODEL, N_EXPERTS), D_MODEL ** -0.5),
        "b_router": nrm(ks[24], (DEPTH, N_EXPERTS), 0.01),
        "w_gate_up": nrm(ks[25], (DEPTH, N_EXPERTS, D_MODEL, 2 * D_FF), D_MODEL ** -0.5),
        "b_gate_up": nrm(ks[26], (DEPTH, N_EXPERTS, 2 * D_FF), 0.01),
        "w_down": nrm(ks[27], (DEPTH, N_EXPERTS, D_FF, D_MODEL), D_FF ** -0.5),
        "b_down": nrm(ks[28], (DEPTH, N_EXPERTS, D_MODEL), 0.01),
        "norm_final_g": 1.0 + nrm(ks[29], (D_MODEL,), 0.02),
    }


def reference(x, meta_tokens, norm_mix_g, w_in, b_in, conv_dw_w, conv_dw_b, conv_ln_g, conv_ln_b,
              w_conv_out, b_conv_out, rnn_conv_w, rnn_conv_b, rg_a_w, rg_a_b, rg_x_w, rg_x_b,
              rg_lambda, w_rnn_out, b_rnn_out, w_out, b_out, norm_ffn_g, w_router, b_router,
              w_gate_up, b_gate_up, w_down, b_down, norm_final_g):
    Bn = x.shape[0]
    meta = jnp.broadcast_to(meta_tokens.astype(x.dtype)[None], (Bn, N_META, D_MODEL))
    h = jnp.concatenate([meta, x], axis=1)
    o1 = 2 * D_CONV
    o2 = o1 + D_RNN
    o3 = o2 + D_RNN
    for l in range(DEPTH):
        xn = rmsnorm(h, norm_mix_g[l])
        proj = xn @ w_in[l] + b_in[l]
        y_conv = conformer_conv_branch(proj[..., :o1], conv_dw_w[l], conv_dw_b[l],
                                       conv_ln_g[l], conv_ln_b[l], w_conv_out[l], b_conv_out[l])
        y_rnn = rglru_branch(proj[..., o1:o2], proj[..., o2:o3], rnn_conv_w[l], rnn_conv_b[l],
                             rg_a_w[l], rg_a_b[l], rg_x_w[l], rg_x_b[l], rg_lambda[l],
                             w_rnn_out[l], b_rnn_out[l])
        gates = jax.nn.sigmoid(proj[..., o3:])
        mixed = gates[..., :D_MODEL] * y_conv + gates[..., D_MODEL:] * y_rnn
        h = h + (mixed @ w_out[l] + b_out[l])
        hn = rmsnorm(h, norm_ffn_g[l])
        h = h + moe_ffn(hn, w_router[l], b_router[l], w_gate_up[l], b_gate_up[l],
                        w_down[l], b_down[l])
    h = rmsnorm(h, norm_final_g)
    return h[:, N_META:]
```

```python
import functools

import jax
import jax.numpy as jnp
from jax import lax
from jax.experimental import pallas as pl
from jax.experimental.pallas import tpu as pltpu

F32 = jnp.float32
BF16 = jnp.bfloat16

EPS = 1e-6
RG_C = 8.0
TOP_K = 4
SWIGLU_ALPHA = 1.702
SWIGLU_LIMIT = 7.0
MOE_BLOCK = 256
RNN_GROUP_HEADS = 4
CONV_HIST = 32
RNN_HIST = 8
LANES = 128
SUBLANES = 8
VMEM_LIMIT = 56 * 1024 * 1024


def _params(sem, vmem=VMEM_LIMIT):
    return pltpu.CompilerParams(dimension_semantics=sem, vmem_limit_bytes=vmem)


def _pick(n, cands):
    for c in cands:
        if n % c == 0:
            return c
    return n


def _const_spec(shape):
    zeros = (0,) * len(shape)
    return pl.BlockSpec(shape, lambda *_: zeros)


def _sigmoid(x):
    return 1.0 / (1.0 + jnp.exp(-x))


def _inproj_body(x_ref, g_ref, w_ref, b_ref, o_ref, xn_ref):
    @pl.when(pl.program_id(1) == 0)
    def _():
        x = x_ref[...]
        ms = jnp.mean(x * x, axis=-1, keepdims=True)
        xn_ref[...] = (x * lax.rsqrt(ms + EPS) * g_ref[...]).astype(BF16)

    acc = jnp.dot(xn_ref[...], w_ref[...], preferred_element_type=F32)
    o_ref[...] = (acc + b_ref[...]).astype(o_ref.dtype)


def _inproj(x2d, g, w, b, tm):
    m, k = x2d.shape
    n = w.shape[1]
    tn = _pick(n, (1024, 512, 256, 128))
    return pl.pallas_call(
        _inproj_body,
        grid=(m // tm, n // tn),
        in_specs=[
            pl.BlockSpec((tm, k), lambda i, j: (i, 0)),
            pl.BlockSpec((1, k), lambda i, j: (0, 0)),
            pl.BlockSpec((k, tn), lambda i, j: (0, j)),
            pl.BlockSpec((1, tn), lambda i, j: (0, j)),
        ],
        out_specs=pl.BlockSpec((tm, tn), lambda i, j: (i, j)),
        out_shape=jax.ShapeDtypeStruct((m, n), BF16),
        scratch_shapes=[pltpu.VMEM((tm, k), BF16)],
        compiler_params=_params(("arbitrary", "arbitrary")),
        name="inproj",
    )(x2d, g, w, b)


def _conv_body(pc_ref, pcm_ref, dww_ref, dwb_ref, lng_ref, lnb_ref, wpw_ref, bpw_ref, o_ref,
               ext_ref, cv_ref, *, tm, dc, width, n_meta, row_chunk):
    i = pl.program_id(1)

    @pl.when(i == 0)
    def _():
        am = pcm_ref[:, 0:dc].astype(F32)
        gm = pcm_ref[:, dc:2 * dc].astype(F32)
        ext_ref[0:CONV_HIST - n_meta, :] = jnp.zeros((CONV_HIST - n_meta, dc), F32)
        ext_ref[CONV_HIST - n_meta:CONV_HIST, :] = am * _sigmoid(gm)

    @pl.when(i > 0)
    def _():
        ext_ref[0:CONV_HIST, :] = ext_ref[tm:tm + CONV_HIST, :]

    a = pc_ref[:, 0:dc].astype(F32)
    g = pc_ref[:, dc:2 * dc].astype(F32)
    ext_ref[CONV_HIST:CONV_HIST + tm, :] = a * _sigmoid(g)

    base = CONV_HIST - (width - 1)

    def col_body(c, carry):
        c0 = pl.multiple_of(c * LANES, LANES)
        for r in range(tm // row_chunk):
            acc = None
            for k in range(width):
                wk = dww_ref[k:k + 1, pl.ds(c0, LANES)]
                r0 = r * row_chunk + base + k
                term = wk * ext_ref[r0:r0 + row_chunk, pl.ds(c0, LANES)]
                acc = term if acc is None else acc + term
            cv_ref[r * row_chunk:(r + 1) * row_chunk, pl.ds(c0, LANES)] = acc + dwb_ref[:, pl.ds(c0, LANES)]
        return carry

    lax.fori_loop(0, dc // LANES, col_body, 0)

    v = cv_ref[...]
    mu = jnp.mean(v, axis=-1, keepdims=True)
    d = v - mu
    var = jnp.mean(d * d, axis=-1, keepdims=True)
    y = d * lax.rsqrt(var + EPS) * lng_ref[...] + lnb_ref[...]
    y = y * _sigmoid(y)
    z = jnp.dot(y.astype(BF16), wpw_ref[...], preferred_element_type=F32) + bpw_ref[...]
    gate = _sigmoid(pc_ref[:, 2 * dc:].astype(F32))
    o_ref[...] = gate * z


def _conv_branch(pc, pcm, dww, dwb, lng, lnb, wpw, bpw, *, batch, seq, tm, d_model):
    dc = dww.shape[1]
    width = dww.shape[0]
    n_meta = pcm.shape[0]
    nt = seq // tm
    body = functools.partial(_conv_body, tm=tm, dc=dc, width=width, n_meta=n_meta,
                             row_chunk=_pick(tm, (128, 64, 32, 16, 8)))
    return pl.pallas_call(
        body,
        grid=(batch, nt),
        in_specs=[
            pl.BlockSpec((tm, pc.shape[1]), lambda b, i: (b * nt + i, 0)),
            _const_spec(pcm.shape),
            _const_spec(dww.shape),
            _const_spec(dwb.shape),
            _const_spec(lng.shape),
            _const_spec(lnb.shape),
            _const_spec(wpw.shape),
            _const_spec(bpw.shape),
        ],
        out_specs=pl.BlockSpec((tm, d_model), lambda b, i: (b * nt + i, 0)),
        out_shape=jax.ShapeDtypeStruct((batch * seq, d_model), F32),
        scratch_shapes=[pltpu.VMEM((tm + CONV_HIST, dc), F32), pltpu.VMEM((tm, dc), F32)],
        compiler_params=_params(("arbitrary", "arbitrary")),
        name="conv_branch",
    )(pc, pcm, dww, dwb, lng, lnb, wpw, bpw)


def _gelu_tanh(x):
    return 0.5 * x * (1.0 + jnp.tanh(0.7978845608028654 * (x + 0.044715 * (x * x * x))))


def _softplus(x):
    return jnp.maximum(x, 0.0) + jnp.log(1.0 + jnp.exp(-jnp.abs(x)))


def _rnn_body(pr_ref, prm_ref, mc_ref, cw_ref, cb_ref, wbd_ref, bg_ref, lam_ref, wo_ref, bo_ref, o_ref,
              ext_ref, hc_ref, a_ref, b_ref, h_ref, *, tm, dr, width, n_meta, gw):
    i = pl.program_id(1)
    n_groups = dr // gw
    c_decay = -RG_C * _softplus(-lam_ref[...])
    row = lax.broadcasted_iota(jnp.int32, (SUBLANES, dr), 0)

    def recurrence(n):
        base = RNN_HIST - (width - 1)
        xc = cb_ref[...] + cw_ref[0:1, :] * ext_ref[base:base + n, :]
        for k in range(1, width):
            xc = xc + cw_ref[k:k + 1, :] * ext_ref[base + k:base + k + n, :]
        xcb = xc.astype(BF16)
        for gi in range(n_groups):
            lo, hi = gi * gw, (gi + 1) * gw
            z = jnp.dot(xcb[:, lo:hi], wbd_ref[gi], preferred_element_type=F32) + bg_ref[gi]
            r = _sigmoid(z[:, 0:gw])
            ig = _sigmoid(z[:, gw:2 * gw])
            a = jnp.exp(c_decay[:, lo:hi] * r)
            mult = jnp.sqrt(jnp.maximum(1.0 - a * a, 0.0))
            a_ref[0:n, lo:hi] = a
            b_ref[0:n, lo:hi] = mult * (ig * xc[:, lo:hi])

        def scan_body(s, carry):
            r0 = pl.multiple_of(s * SUBLANES, SUBLANES)
            av = a_ref[pl.ds(r0, SUBLANES), :]
            bv = b_ref[pl.ds(r0, SUBLANES), :]
            for d in (1, 2, 4):
                keep = row >= d
                a_s = pltpu.roll(av, d, axis=0)
                b_s = pltpu.roll(bv, d, axis=0)
                bv = jnp.where(keep, av * b_s + bv, bv)
                av = jnp.where(keep, av * a_s, av)
            h = av * carry + bv
            h_ref[pl.ds(r0, SUBLANES), :] = h
            return jnp.broadcast_to(h[SUBLANES - 1:SUBLANES, :], (SUBLANES, dr))

        hc_ref[...] = lax.fori_loop(0, n // SUBLANES, scan_body, hc_ref[...])

    @pl.when(i == 0)
    def _():
        ext_ref[0:RNN_HIST, :] = jnp.zeros((RNN_HIST, dr), F32)
        ext_ref[RNN_HIST:RNN_HIST + n_meta, :] = prm_ref[:, 0:dr].astype(F32)
        hc_ref[...] = jnp.zeros((SUBLANES, dr), F32)
        recurrence(n_meta)
        ext_ref[0:RNN_HIST, :] = ext_ref[n_meta:n_meta + RNN_HIST, :]

    @pl.when(i > 0)
    def _():
        ext_ref[0:RNN_HIST, :] = ext_ref[tm:tm + RNN_HIST, :]

    ext_ref[RNN_HIST:RNN_HIST + tm, :] = pr_ref[:, 0:dr].astype(F32)
    recurrence(tm)
    y = h_ref[...] * _gelu_tanh(pr_ref[:, dr:2 * dr].astype(F32))
    z = jnp.dot(y.astype(BF16), wo_ref[...], preferred_element_type=F32) + bo_ref[...]
    gate = _sigmoid(pr_ref[:, 2 * dr:].astype(F32))
    o_ref[...] = mc_ref[...] + gate * z


def _rnn_branch(pr, prm, mc, cw, cb, wbd, bg, lam, wo, bo, *, batch, seq, tm, d_model):
    dr = cw.shape[1]
    width = cw.shape[0]
    n_meta = prm.shape[0]
    gw = wbd.shape[1]
    nt = seq // tm
    body = functools.partial(_rnn_body, tm=tm, dr=dr, width=width, n_meta=n_meta, gw=gw)
    return pl.pallas_call(
        body,
        grid=(batch, nt),
        in_specs=[
            pl.BlockSpec((tm, pr.shape[1]), lambda b, i: (b * nt + i, 0)),
            _const_spec(prm.shape),
            pl.BlockSpec((tm, d_model), lambda b, i: (b * nt + i, 0)),
            _const_spec(cw.shape),
            _const_spec(cb.shape),
            _const_spec(wbd.shape),
            _const_spec(bg.shape),
            _const_spec(lam.shape),
            _const_spec(wo.shape),
            _const_spec(bo.shape),
        ],
        out_specs=pl.BlockSpec((tm, d_model), lambda b, i: (b * nt + i, 0)),
        out_shape=jax.ShapeDtypeStruct((batch * seq, d_model), F32),
        scratch_shapes=[
            pltpu.VMEM((tm + RNN_HIST, dr), F32),
            pltpu.VMEM((SUBLANES, dr), F32),
            pltpu.VMEM((tm, dr), F32),
            pltpu.VMEM((tm, dr), F32),
            pltpu.VMEM((tm, dr), F32),
        ],
        compiler_params=_params(("arbitrary", "arbitrary")),
        name="rnn_branch",
    )(pr, prm, mc, cw, cb, wbd, bg, lam, wo, bo)


def _split_bf16(v):
    hi = v.astype(BF16)
    lo = (v - hi.astype(F32)).astype(BF16)
    return hi, lo


def _post_body(x_ref, mix_ref, wout_ref, bout_ref, gffn_ref, wr_ref, br_ref,
               h1_ref, hnp_ref, route_ref, cnt_ref, run_ref, *, tm, d_model, n_exp):
    step = pl.program_id(0)

    @pl.when(step == 0)
    def _():
        run_ref[...] = jnp.zeros((1, n_exp), F32)

    h1 = x_ref[...] + jnp.dot(mix_ref[...].astype(BF16), wout_ref[...], preferred_element_type=F32) + bout_ref[...]
    h1_ref[...] = h1
    ms = jnp.mean(h1 * h1, axis=-1, keepdims=True)
    hn = h1 * lax.rsqrt(ms + EPS) * gffn_ref[...]

    half = d_model // 2
    hb = hn.astype(BF16).astype(F32)
    lo_bits = lax.shift_right_logical(pltpu.bitcast(hb[:, 0:half], jnp.uint32), jnp.uint32(16))
    hi_bits = pltpu.bitcast(hb[:, half:], jnp.uint32) & jnp.uint32(0xFFFF0000)
    hnp_ref[...] = lo_bits | hi_bits

    hn_hi, hn_lo = _split_bf16(hn)
    w_hi, w_lo = _split_bf16(wr_ref[...])
    logits = (jnp.dot(hn_hi, w_hi, preferred_element_type=F32)
              + jnp.dot(hn_hi, w_lo, preferred_element_type=F32)
              + jnp.dot(hn_lo, w_hi, preferred_element_type=F32)) + br_ref[...]

    lane = lax.broadcasted_iota(jnp.int32, (tm, n_exp), 1).astype(F32)
    work = logits
    vals, idxs = [], []
    for _ in range(TOP_K):
        m = jnp.max(work, axis=-1, keepdims=True)
        ik = jnp.min(jnp.where(work == m, lane, float(n_exp)), axis=-1, keepdims=True)
        vals.append(m)
        idxs.append(ik)
        work = jnp.where(lane == ik, -jnp.inf, work)
    exps = [jnp.exp(v - vals[0]) for v in vals]
    denom = exps[0] + exps[1] + exps[2] + exps[3]

    onehots = [(lane == ik).astype(F32) for ik in idxs]
    sel = onehots[0] + onehots[1] + onehots[2] + onehots[3]
    r_i = lax.broadcasted_iota(jnp.int32, (tm, tm), 0)
    c_i = lax.broadcasted_iota(jnp.int32, (tm, tm), 1)
    strict_lower = (c_i < r_i).astype(BF16)
    before = jnp.dot(strict_lower, sel.astype(BF16), preferred_element_type=F32) + run_ref[...]
    ranks = [jnp.sum(before * oh, axis=-1, keepdims=True) for oh in onehots]
    run_ref[...] = run_ref[...] + jnp.sum(sel, axis=0, keepdims=True)
    cnt_ref[...] = run_ref[...]

    out_lane = lax.broadcasted_iota(jnp.int32, (tm, LANES), 1)
    route = jnp.zeros((tm, LANES), F32)
    for k in range(TOP_K):
        route = jnp.where(out_lane == k, idxs[k], route)
        route = jnp.where(out_lane == TOP_K + k, ranks[k], route)
        route = jnp.where(out_lane == 2 * TOP_K + k, exps[k] / denom, route)
    route_ref[...] = route


def _post(x2d, mixed, wout, bout, gffn, wr, br, *, tm):
    t, d = x2d.shape
    n_exp = wr.shape[1]
    body = functools.partial(_post_body, tm=tm, d_model=d, n_exp=n_exp)
    row_spec = pl.BlockSpec((tm, d), lambda i: (i, 0))
    return pl.pallas_call(
        body,
        grid=(t // tm,),
        in_specs=[row_spec, row_spec, _const_spec(wout.shape), _const_spec(bout.shape), _const_spec(gffn.shape),
                  _const_spec(wr.shape), _const_spec(br.shape)],
        out_specs=[row_spec, pl.BlockSpec((tm, d // 2), lambda i: (i, 0)), pl.BlockSpec((tm, LANES), lambda i: (i, 0)),
                   _const_spec((1, n_exp))],
        out_shape=[jax.ShapeDtypeStruct((t, d), F32), jax.ShapeDtypeStruct((t, d // 2), jnp.uint32),
                   jax.ShapeDtypeStruct((t, LANES), F32), jax.ShapeDtypeStruct((1, n_exp), F32)],
        scratch_shapes=[pltpu.VMEM((1, n_exp), F32)],
        compiler_params=_params(("arbitrary",)),
        name="post_router",
    )(x2d, mixed, wout, bout, gffn, wr, br)


def _gather_body(dest_ref, zlo_ref, zhi_ref, hnp_ref, xbuf_ref, zrow_ref, sems, zsem, *, tg, n_exp):
    s = pl.program_id(0)
    n = pl.num_programs(0)
    slot = s % 2

    def wait_step(sl):
        pltpu.make_async_copy(hnp_ref.at[pl.ds(0, tg * TOP_K)], xbuf_ref.at[pl.ds(0, tg * TOP_K)], sems.at[sl]).wait()

    @pl.when(s == 0)
    def _():
        zrow_ref[...] = jnp.zeros(zrow_ref.shape, zrow_ref.dtype)

        def pad_copy(r):
            return pltpu.make_async_copy(zrow_ref, xbuf_ref.at[pl.ds(r, 1)], zsem)

        def expert_body(e, carry):
            def start(r, c2):
                pad_copy(r).start()
                return c2

            def wait(r, c2):
                pad_copy(r).wait()
                return c2

            lax.fori_loop(zlo_ref[e], zhi_ref[e], start, 0)
            lax.fori_loop(zlo_ref[e], zhi_ref[e], wait, 0)
            return carry

        lax.fori_loop(0, n_exp, expert_body, 0)

    def issue(j, carry):
        tok = s * tg + j
        for k in range(TOP_K):
            dst = dest_ref[tok * TOP_K + k]
            pltpu.make_async_copy(hnp_ref.at[pl.ds(tok, 1)], xbuf_ref.at[pl.ds(dst, 1)], sems.at[slot]).start()
        return carry

    lax.fori_loop(0, tg, issue, 0)

    @pl.when(s > 0)
    def _():
        wait_step(1 - slot)

    @pl.when(s == n - 1)
    def _():
        wait_step(slot)


def _gather(dest, zlo, zhi, hnp, n_rows):
    t, width = hnp.shape
    tg = _pick(t, (256, 128, 64))
    body = functools.partial(_gather_body, tg=tg, n_exp=zlo.shape[0])
    return pl.pallas_call(
        body,
        grid_spec=pltpu.PrefetchScalarGridSpec(
            num_scalar_prefetch=3,
            grid=(t // tg,),
            in_specs=[pl.BlockSpec(memory_space=pl.ANY)],
            out_specs=pl.BlockSpec(memory_space=pl.ANY),
            scratch_shapes=[pltpu.VMEM((1, width), jnp.uint32), pltpu.SemaphoreType.DMA((2,)), pltpu.SemaphoreType.DMA(())],
        ),
        out_shape=jax.ShapeDtypeStruct((n_rows, width), jnp.uint32),
        compiler_params=_params(("arbitrary",)),
        name="dispatch_gather",
    )(dest, zlo, zhi, hnp)


def _moe_body(be_ref, nu_ref, x_ref, wgu_ref, bgu_ref, wdn_ref, bdn_ref, o_ref, *, d_ff, tf):
    blk = pl.program_id(0)

    @pl.when(blk < nu_ref[0])
    def _():
        w = x_ref[...]
        lo = pltpu.bitcast(lax.shift_left(w, jnp.uint32(16)), F32)
        hi = pltpu.bitcast(w & jnp.uint32(0xFFFF0000), F32)
        xb = jnp.concatenate([lo, hi], axis=1).astype(BF16)
        acc = None
        for j in range(d_ff // tf):
            g = jnp.dot(xb, wgu_ref[0, :, j * tf:(j + 1) * tf], preferred_element_type=F32) + bgu_ref[0, :, j * tf:(j + 1) * tf]
            u = jnp.dot(xb, wgu_ref[0, :, d_ff + j * tf:d_ff + (j + 1) * tf], preferred_element_type=F32) \
                + bgu_ref[0, :, d_ff + j * tf:d_ff + (j + 1) * tf]
            g = jnp.minimum(g, SWIGLU_LIMIT)
            u = jnp.clip(u, -SWIGLU_LIMIT, SWIGLU_LIMIT)
            act = g * _sigmoid(SWIGLU_ALPHA * g) * (u + 1.0)
            part = jnp.dot(act.astype(BF16), wdn_ref[0, j * tf:(j + 1) * tf, :], preferred_element_type=F32)
            acc = part if acc is None else acc + part
        o_ref[...] = acc + bdn_ref[0]

    @pl.when(blk >= nu_ref[0])
    def _():
        o_ref[...] = jnp.zeros(o_ref.shape, o_ref.dtype)


def _moe(block_expert, n_used, xbuf, wgu, bgu, wdn, bdn, *, n_blocks):
    n_exp, d, two_ff = wgu.shape
    d_ff = two_ff // 2
    tf = _pick(d_ff, (512, 256, 128))
    body = functools.partial(_moe_body, d_ff=d_ff, tf=tf)

    def blk_map(i, be, nu):
        return (jnp.minimum(i, nu[0] - 1), 0)

    def exp_map(i, be, nu):
        return (be[jnp.minimum(i, nu[0] - 1)], 0, 0)

    return pl.pallas_call(
        body,
        grid_spec=pltpu.PrefetchScalarGridSpec(
            num_scalar_prefetch=2,
            grid=(n_blocks,),
            in_specs=[
                pl.BlockSpec((MOE_BLOCK, d // 2), blk_map),
                pl.BlockSpec((1, d, two_ff), exp_map),
                pl.BlockSpec((1, 1, two_ff), exp_map),
                pl.BlockSpec((1, d_ff, d), exp_map, pipeline_mode=pl.Buffered(1)),
                pl.BlockSpec((1, 1, d), exp_map),
            ],
            out_specs=pl.BlockSpec((MOE_BLOCK, d), lambda i, be, nu: (i, 0)),
        ),
        out_shape=jax.ShapeDtypeStruct((n_blocks * MOE_BLOCK, d), F32),
        compiler_params=_params(("arbitrary",)),
        name="moe_ffn",
    )(block_expert, n_used, xbuf, wgu, bgu, wdn, bdn)


def _combine_body(dest_ref, y_ref, h1_ref, gates_ref, gfin_ref, o_ref, ybuf_ref, sems, *, tc):
    i = pl.program_id(0)
    n = pl.num_programs(0)

    def issue(step, slot):
        def body(j, carry):
            for k in range(TOP_K):
                d = dest_ref[(step * tc + j) * TOP_K + k]
                pltpu.make_async_copy(y_ref.at[pl.ds(d, 1)], ybuf_ref.at[slot, k, pl.ds(j, 1)], sems.at[slot]).start()
            return carry

        lax.fori_loop(0, tc, body, 0)

    @pl.when(i == 0)
    def _():
        issue(0, 0)

    @pl.when(i + 1 < n)
    def _():
        issue(i + 1, (i + 1) % 2)

    slot = i % 2
    for k in range(TOP_K):
        pltpu.make_async_copy(y_ref.at[pl.ds(0, tc)], ybuf_ref.at[slot, k], sems.at[slot]).wait()
    acc = h1_ref[...]
    gates = gates_ref[...]
    for k in range(TOP_K):
        acc = acc + gates[:, 2 * TOP_K + k:2 * TOP_K + k + 1] * ybuf_ref[slot, k]
    ms = jnp.mean(acc * acc, axis=-1, keepdims=True)
    o_ref[...] = acc * lax.rsqrt(ms + EPS) * gfin_ref[...]


def _combine(dest, ybuf, h1, route, gfin, *, tc):
    t, d = h1.shape
    body = functools.partial(_combine_body, tc=tc)
    return pl.pallas_call(
        body,
        grid_spec=pltpu.PrefetchScalarGridSpec(
            num_scalar_prefetch=1,
            grid=(t // tc,),
            in_specs=[
                pl.BlockSpec(memory_space=pl.ANY),
                pl.BlockSpec((tc, d), lambda i, dref: (i, 0)),
                pl.BlockSpec((tc, LANES), lambda i, dref: (i, 0)),
                pl.BlockSpec((1, d), lambda i, dref: (0, 0)),
            ],
            out_specs=pl.BlockSpec((tc, d), lambda i, dref: (i, 0)),
            scratch_shapes=[pltpu.VMEM((2, TOP_K, tc, d), F32), pltpu.SemaphoreType.DMA((2,))],
        ),
        out_shape=jax.ShapeDtypeStruct((t, d), F32),
        compiler_params=_params(("arbitrary",)),
        name="combine_norm",
    )(dest, ybuf, h1, route, gfin)


def _block_diag_gates(wa, ba, wx, bx):
    n_heads, hd, _ = wa.shape
    g = RNN_GROUP_HEADS
    n_groups = n_heads // g
    eye = jnp.eye(g, dtype=wa.dtype)

    def bd(w):
        w = w.reshape(n_groups, g, hd, hd)
        return jnp.einsum("nghj,gk->nghkj", w, eye).reshape(n_groups, g * hd, g * hd)

    wbd = jnp.concatenate([bd(wa), bd(wx)], axis=-1).astype(BF16)
    bg = jnp.concatenate([ba.reshape(n_groups, 1, g * hd), bx.reshape(n_groups, 1, g * hd)], axis=-1)
    return wbd, bg


def kernel(x, meta_tokens, norm_mix_g, w_in, b_in, conv_dw_w, conv_dw_b, conv_ln_g, conv_ln_b, w_conv_out, b_conv_out, rnn_conv_w, rnn_conv_b, rg_a_w, rg_a_b, rg_x_w, rg_x_b, rg_lambda, w_rnn_out, b_rnn_out, w_out, b_out, norm_ffn_g, w_router, b_router, w_gate_up, b_gate_up, w_down, b_down, norm_final_g):
    batch, seq, d = x.shape
    assert norm_mix_g.shape[0] == 1, "single-layer block"
    dc = conv_dw_w.shape[2]
    dr = rnn_conv_w.shape[2]
    n_exp = w_router.shape[2]
    t = batch * seq
    o1, o2, o3 = 2 * dc, 2 * dc + dr, 2 * dc + 2 * dr

    w_in0, b_in0 = w_in[0], b_in[0]
    w_c = jnp.concatenate([w_in0[:, :o1], w_in0[:, o3:o3 + d]], axis=1).astype(BF16)
    b_c = jnp.concatenate([b_in0[:o1], b_in0[o3:o3 + d]])[None]
    w_r = jnp.concatenate([w_in0[:, o1:o3], w_in0[:, o3 + d:]], axis=1).astype(BF16)
    b_r = jnp.concatenate([b_in0[o1:o3], b_in0[o3 + d:]])[None]
    wbd, bg = _block_diag_gates(rg_a_w[0], rg_a_b[0], rg_x_w[0], rg_x_b[0])
    g_mix = norm_mix_g[0][None]

    x2d = x.reshape(t, d)
    tm_in = _pick(t, (512, 256, 128))
    tm = _pick(seq, (256, 128))

    pc = _inproj(x2d, g_mix, w_c, b_c, tm_in)
    pr = _inproj(x2d, g_mix, w_r, b_r, tm_in)
    pcm = _inproj(meta_tokens, g_mix, w_c, b_c, meta_tokens.shape[0])
    prm = _inproj(meta_tokens, g_mix, w_r, b_r, meta_tokens.shape[0])

    mc = _conv_branch(pc, pcm, conv_dw_w[0], conv_dw_b[0][None], conv_ln_g[0][None], conv_ln_b[0][None],
                      w_conv_out[0].astype(BF16), b_conv_out[0][None], batch=batch, seq=seq, tm=tm, d_model=d)
    mixed = _rnn_branch(pr, prm, mc, rnn_conv_w[0], rnn_conv_b[0][None], wbd, bg, rg_lambda[0][None],
                        w_rnn_out[0].astype(BF16), b_rnn_out[0][None], batch=batch, seq=seq, tm=tm, d_model=d)
    wr = jnp.pad(w_router[0], ((0, 0), (0, LANES - n_exp)))
    br = jnp.pad(b_router[0], (0, LANES - n_exp), constant_values=-jnp.inf)[None]
    h1, hnp, route, counts = _post(x2d, mixed, w_out[0].astype(BF16), b_out[0][None], norm_ffn_g[0][None],
                                   wr, br, tm=tm)

    counts = counts[0, :n_exp].astype(jnp.int32)
    padded = (counts + MOE_BLOCK - 1) // MOE_BLOCK * MOE_BLOCK
    pad_end = jnp.cumsum(padded)
    pad_start = pad_end - padded
    n_blocks = (t * TOP_K + n_exp * (MOE_BLOCK - 1)) // MOE_BLOCK
    n_used = (pad_end[-1] // MOE_BLOCK).astype(jnp.int32)[None]
    block_expert = jnp.minimum(
        jnp.sum((jnp.arange(n_blocks, dtype=jnp.int32)[:, None] * MOE_BLOCK >= pad_end[None, :]).astype(jnp.int32), axis=1),
        n_exp - 1).astype(jnp.int32)
    idx = route[:, 0:TOP_K].astype(jnp.int32)
    rank = route[:, TOP_K:2 * TOP_K].astype(jnp.int32)
    start_of = jnp.sum(jnp.where(idx[..., None] == jnp.arange(n_exp, dtype=jnp.int32), pad_start, 0), axis=-1)
    dest = (start_of + rank).reshape(t * TOP_K)

    n_rows = n_blocks * MOE_BLOCK
    zero_lo = jnp.concatenate([pad_start + counts, pad_end[-1:]]).astype(jnp.int32)
    zero_hi = jnp.concatenate([pad_end, jnp.full((1,), n_rows, jnp.int32)]).astype(jnp.int32)
    xbuf = _gather(dest, zero_lo, zero_hi, hnp, n_rows)
    ybuf = _moe(block_expert, n_used, xbuf, w_gate_up[0].astype(BF16), b_gate_up[0][:, None, :],
                w_down[0].astype(BF16), b_down[0][:, None, :], n_blocks=n_blocks)
    out = _combine(dest, ybuf, h1, route, norm_final_g[None], tc=_pick(t, (128, 64)))
    return out.reshape(batch, seq, d)
```

```python
import functools

import jax
import jax.numpy as jnp
from jax import lax
from jax.experimental import pallas as pl
from jax.experimental.pallas import tpu as pltpu

F32 = jnp.float32
BF16 = jnp.bfloat16

EPS = 1e-6
RG_C = 8.0
TOP_K = 4
SWIGLU_ALPHA = 1.702
SWIGLU_LIMIT = 7.0
MOE_BLOCK = 256
RNN_GROUP_HEADS = 4
CONV_HIST = 32
CONV_STRIDE = 4
RNN_HIST = 8
DMA_UNROLL = 4
LANES = 128
SUBLANES = 8
VMEM_LIMIT = 56 * 1024 * 1024


def _params(sem, vmem=VMEM_LIMIT):
    return pltpu.CompilerParams(dimension_semantics=sem, vmem_limit_bytes=vmem)


def _pick(n, cands):
    for c in cands:
        if n % c == 0:
            return c
    return n


def _const_spec(shape):
    zeros = (0,) * len(shape)
    return pl.BlockSpec(shape, lambda *_: zeros)


def _sigmoid(x):
    return 0.5 * jnp.tanh(0.5 * x) + 0.5


def _inproj_body(x_ref, g_ref, w_ref, b_ref, o_ref, xn_ref):
    @pl.when(pl.program_id(1) == 0)
    def _():
        x = x_ref[...]
        ms = jnp.mean(x * x, axis=-1, keepdims=True)
        xn_ref[...] = (x * lax.rsqrt(ms + EPS) * g_ref[...]).astype(BF16)

    acc = jnp.dot(xn_ref[...], w_ref[...], preferred_element_type=F32)
    o_ref[...] = (acc + b_ref[...]).astype(o_ref.dtype)


def _inproj(x2d, g, w, b, tm):
    m, k = x2d.shape
    n = w.shape[1]
    tn = _pick(n, (1024, 512, 256, 128))
    return pl.pallas_call(
        _inproj_body,
        grid=(m // tm, n // tn),
        in_specs=[
            pl.BlockSpec((tm, k), lambda i, j: (i, 0)),
            pl.BlockSpec((1, k), lambda i, j: (0, 0)),
            pl.BlockSpec((k, tn), lambda i, j: (0, j)),
            pl.BlockSpec((1, tn), lambda i, j: (0, j)),
        ],
        out_specs=pl.BlockSpec((tm, tn), lambda i, j: (i, j)),
        out_shape=jax.ShapeDtypeStruct((m, n), BF16),
        scratch_shapes=[pltpu.VMEM((tm, k), BF16)],
        compiler_params=_params(("arbitrary", "arbitrary")),
        name="inproj",
    )(x2d, g, w, b)


def _conv_body(pc_ref, pcm_ref, dww_ref, dwb_ref, lng_ref, lnb_ref, wpw_ref, bpw_ref, o_ref,
               ext_ref, cv_ref, yb_ref, *, tm, dc, width, n_meta):
    i = pl.program_id(1)
    nc = dc // LANES

    def cols(c):
        return slice(c * LANES, (c + 1) * LANES)

    @pl.when(i == 0)
    def _():
        for c in range(nc):
            am = pcm_ref[:, cols(c)].astype(F32)
            gm = pcm_ref[:, dc + c * LANES:dc + (c + 1) * LANES].astype(F32)
            ext_ref[c, 0:CONV_HIST - n_meta, :] = jnp.zeros((CONV_HIST - n_meta, LANES), F32)
            ext_ref[c, CONV_HIST - n_meta:CONV_HIST, :] = am * _sigmoid(gm)

    @pl.when(i > 0)
    def _():
        ext_ref[:, 0:CONV_HIST, :] = ext_ref[:, tm:tm + CONV_HIST, :]

    for c in range(nc):
        a = pc_ref[:, cols(c)].astype(F32)
        g = pc_ref[:, dc + c * LANES:dc + (c + 1) * LANES].astype(F32)
        ext_ref[c, CONV_HIST:CONV_HIST + tm, :] = a * _sigmoid(g)

    base = CONV_HIST - (width - 1)
    sub = SUBLANES * CONV_STRIDE

    def col_body(c, carry):
        taps = [jnp.broadcast_to(dww_ref[c, k:k + 1, :], (SUBLANES, LANES)) for k in range(width)]
        for j in range(tm // sub):
            for q in range(CONV_STRIDE):
                acc = None
                for k in range(width):
                    xv = ext_ref[c, pl.ds(j * sub + q + base + k, SUBLANES, stride=CONV_STRIDE), :]
                    acc = taps[k] * xv if acc is None else acc + taps[k] * xv
                cv_ref[c, pl.ds(j * sub + q, SUBLANES, stride=CONV_STRIDE), :] = acc
        return carry

    lax.fori_loop(0, nc, col_body, 0)

    total = None
    for c in range(nc):
        vc = cv_ref[c] + dwb_ref[:, cols(c)]
        cv_ref[c] = vc
        total = vc if total is None else total + vc
    mu = jnp.sum(total, axis=-1, keepdims=True) * (1.0 / dc)
    sq = None
    for c in range(nc):
        dcen = cv_ref[c] - mu
        sq = dcen * dcen if sq is None else sq + dcen * dcen
    rstd = lax.rsqrt(jnp.sum(sq, axis=-1, keepdims=True) * (1.0 / dc) + EPS)
    for c in range(nc):
        y = (cv_ref[c] - mu) * rstd * lng_ref[:, cols(c)] + lnb_ref[:, cols(c)]
        yb_ref[:, cols(c)] = (y * _sigmoid(y)).astype(BF16)
    z = jnp.dot(yb_ref[...], wpw_ref[...], preferred_element_type=F32) + bpw_ref[...]
    gate = _sigmoid(pc_ref[:, 2 * dc:].astype(F32))
    o_ref[...] = gate * z


def _conv_branch(pc, pcm, dww, dwb, lng, lnb, wpw, bpw, *, batch, seq, tm, d_model):
    width, dc = dww.shape
    nc = dc // LANES
    n_meta = pcm.shape[0]
    nt = seq // tm
    wpad = -(-width // SUBLANES) * SUBLANES
    dww = jnp.pad(dww, ((0, wpad - width), (0, 0))).reshape(wpad, nc, LANES).transpose(1, 0, 2)
    body = functools.partial(_conv_body, tm=tm, dc=dc, width=width, n_meta=n_meta)
    return pl.pallas_call(
        body,
        grid=(batch, nt),
        in_specs=[
            pl.BlockSpec((tm, pc.shape[1]), lambda b, i: (b * nt + i, 0)),
            _const_spec(pcm.shape),
            _const_spec(dww.shape),
            _const_spec(dwb.shape),
            _const_spec(lng.shape),
            _const_spec(lnb.shape),
            _const_spec(wpw.shape),
            _const_spec(bpw.shape),
        ],
        out_specs=pl.BlockSpec((tm, d_model), lambda b, i: (b * nt + i, 0)),
        out_shape=jax.ShapeDtypeStruct((batch * seq, d_model), F32),
        scratch_shapes=[pltpu.VMEM((nc, tm + CONV_HIST, LANES), F32), pltpu.VMEM((nc, tm, LANES), F32),
                        pltpu.VMEM((tm, dc), BF16)],
        compiler_params=_params(("arbitrary", "arbitrary")),
        name="conv_branch",
    )(pc, pcm, dww, dwb, lng, lnb, wpw, bpw)


def _gelu_tanh(x):
    return 0.5 * x * (1.0 + jnp.tanh(0.7978845608028654 * (x + 0.044715 * (x * x * x))))


def _softplus(x):
    return jnp.maximum(x, 0.0) + jnp.log(1.0 + jnp.exp(-jnp.abs(x)))


def _rnn_body(pr_ref, prm_ref, mc_ref, cw_ref, cb_ref, wbd_ref, bg_ref, lam_ref, wo_ref, bo_ref, o_ref,
              ext_ref, hc_ref, a_ref, b_ref, h_ref, *, tm, dr, width, n_meta, gw):
    i = pl.program_id(1)
    n_groups = dr // gw
    c_decay = -RG_C * _softplus(-lam_ref[...])
    row = lax.broadcasted_iota(jnp.int32, (SUBLANES, dr), 0)

    def recurrence(n):
        base = RNN_HIST - (width - 1)
        xc = cb_ref[...] + cw_ref[0:1, :] * ext_ref[base:base + n, :]
        for k in range(1, width):
            xc = xc + cw_ref[k:k + 1, :] * ext_ref[base + k:base + k + n, :]
        xcb = xc.astype(BF16)
        for gi in range(n_groups):
            lo, hi = gi * gw, (gi + 1) * gw
            z = jnp.dot(xcb[:, lo:hi], wbd_ref[gi], preferred_element_type=F32) + bg_ref[gi]
            r = _sigmoid(z[:, 0:gw])
            ig = _sigmoid(z[:, gw:2 * gw])
            a = jnp.exp(c_decay[:, lo:hi] * r)
            mult = jnp.sqrt(jnp.maximum(1.0 - a * a, 0.0))
            a_ref[0:n, lo:hi] = a
            b_ref[0:n, lo:hi] = mult * (ig * xc[:, lo:hi])

        def scan_body(s, carry):
            r0 = pl.multiple_of(s * SUBLANES, SUBLANES)
            av = a_ref[pl.ds(r0, SUBLANES), :]
            bv = b_ref[pl.ds(r0, SUBLANES), :]
            for d in (1, 2, 4):
                keep = row >= d
                a_s = pltpu.roll(av, d, axis=0)
                b_s = pltpu.roll(bv, d, axis=0)
                bv = jnp.where(keep, av * b_s + bv, bv)
                av = jnp.where(keep, av * a_s, av)
            h = av * carry + bv
            h_ref[pl.ds(r0, SUBLANES), :] = h
            return jnp.broadcast_to(h[SUBLANES - 1:SUBLANES, :], (SUBLANES, dr))

        hc_ref[...] = lax.fori_loop(0, n // SUBLANES, scan_body, hc_ref[...])

    @pl.when(i == 0)
    def _():
        ext_ref[0:RNN_HIST, :] = jnp.zeros((RNN_HIST, dr), F32)
        ext_ref[RNN_HIST:RNN_HIST + n_meta, :] = prm_ref[:, 0:dr].astype(F32)
        hc_ref[...] = jnp.zeros((SUBLANES, dr), F32)
        recurrence(n_meta)
        ext_ref[0:RNN_HIST, :] = ext_ref[n_meta:n_meta + RNN_HIST, :]

    @pl.when(i > 0)
    def _():
        ext_ref[0:RNN_HIST, :] = ext_ref[tm:tm + RNN_HIST, :]

    ext_ref[RNN_HIST:RNN_HIST + tm, :] = pr_ref[:, 0:dr].astype(F32)
    recurrence(tm)
    y = h_ref[...] * _gelu_tanh(pr_ref[:, dr:2 * dr].astype(F32))
    z = jnp.dot(y.astype(BF16), wo_ref[...], preferred_element_type=F32) + bo_ref[...]
    gate = _sigmoid(pr_ref[:, 2 * dr:].astype(F32))
    o_ref[...] = mc_ref[...] + gate * z


def _rnn_branch(pr, prm, mc, cw, cb, wbd, bg, lam, wo, bo, *, batch, seq, tm, d_model):
    dr = cw.shape[1]
    width = cw.shape[0]
    n_meta = prm.shape[0]
    gw = wbd.shape[1]
    nt = seq // tm
    body = functools.partial(_rnn_body, tm=tm, dr=dr, width=width, n_meta=n_meta, gw=gw)
    return pl.pallas_call(
        body,
        grid=(batch, nt),
        in_specs=[
            pl.BlockSpec((tm, pr.shape[1]), lambda b, i: (b * nt + i, 0)),
            _const_spec(prm.shape),
            pl.BlockSpec((tm, d_model), lambda b, i: (b * nt + i, 0)),
            _const_spec(cw.shape),
            _const_spec(cb.shape),
            _const_spec(wbd.shape),
            _const_spec(bg.shape),
            _const_spec(lam.shape),
            _const_spec(wo.shape),
            _const_spec(bo.shape),
        ],
        out_specs=pl.BlockSpec((tm, d_model), lambda b, i: (b * nt + i, 0)),
        out_shape=jax.ShapeDtypeStruct((batch * seq, d_model), F32),
        scratch_shapes=[
            pltpu.VMEM((tm + RNN_HIST, dr), F32),
            pltpu.VMEM((SUBLANES, dr), F32),
            pltpu.VMEM((tm, dr), F32),
            pltpu.VMEM((tm, dr), F32),
            pltpu.VMEM((tm, dr), F32),
        ],
        compiler_params=_params(("arbitrary", "arbitrary")),
        name="rnn_branch",
    )(pr, prm, mc, cw, cb, wbd, bg, lam, wo, bo)


def _split_bf16(v):
    hi = v.astype(BF16)
    lo = (v - hi.astype(F32)).astype(BF16)
    return hi, lo


def _post_body(x_ref, mix_ref, wout_ref, bout_ref, gffn_ref, wr_ref, br_ref,
               h1_ref, hnp_ref, route_ref, cnt_ref, run_ref, *, tm, d_model, n_exp):
    step = pl.program_id(0)

    @pl.when(step == 0)
    def _():
        run_ref[...] = jnp.zeros((1, n_exp), F32)

    h1 = x_ref[...] + jnp.dot(mix_ref[...].astype(BF16), wout_ref[...], preferred_element_type=F32) + bout_ref[...]
    h1_ref[...] = h1
    ms = jnp.mean(h1 * h1, axis=-1, keepdims=True)
    hn = h1 * lax.rsqrt(ms + EPS) * gffn_ref[...]

    half = d_model // 2
    hb = hn.astype(BF16).astype(F32)
    lo_bits = lax.shift_right_logical(pltpu.bitcast(hb[:, 0:half], jnp.uint32), jnp.uint32(16))
    hi_bits = pltpu.bitcast(hb[:, half:], jnp.uint32) & jnp.uint32(0xFFFF0000)
    hnp_ref[...] = lo_bits | hi_bits

    hn_hi, hn_lo = _split_bf16(hn)
    w_hi, w_lo = _split_bf16(wr_ref[...])
    logits = (jnp.dot(hn_hi, w_hi, preferred_element_type=F32)
              + jnp.dot(hn_hi, w_lo, preferred_element_type=F32)
              + jnp.dot(hn_lo, w_hi, preferred_element_type=F32)) + br_ref[...]

    lane = lax.broadcasted_iota(jnp.int32, (tm, n_exp), 1).astype(F32)
    work = logits
    vals, idxs = [], []
    for _ in range(TOP_K):
        m = jnp.max(work, axis=-1, keepdims=True)
        ik = jnp.min(jnp.where(work == m, lane, float(n_exp)), axis=-1, keepdims=True)
        vals.append(m)
        idxs.append(ik)
        work = jnp.where(lane == ik, -jnp.inf, work)
    exps = [jnp.exp(v - vals[0]) for v in vals]
    denom = exps[0] + exps[1] + exps[2] + exps[3]

    onehots = [(lane == ik).astype(F32) for ik in idxs]
    sel = onehots[0] + onehots[1] + onehots[2] + onehots[3]
    r_i = lax.broadcasted_iota(jnp.int32, (tm, tm), 0)
    c_i = lax.broadcasted_iota(jnp.int32, (tm, tm), 1)
    strict_lower = (c_i < r_i).astype(BF16)
    before = jnp.dot(strict_lower, sel.astype(BF16), preferred_element_type=F32) + run_ref[...]
    ranks = [jnp.sum(before * oh, axis=-1, keepdims=True) for oh in onehots]
    run_ref[...] = run_ref[...] + jnp.sum(sel, axis=0, keepdims=True)
    cnt_ref[...] = run_ref[...]

    out_lane = lax.broadcasted_iota(jnp.int32, (tm, LANES), 1)
    route = jnp.zeros((tm, LANES), F32)
    for k in range(TOP_K):
        route = jnp.where(out_lane == k, idxs[k], route)
        route = jnp.where(out_lane == TOP_K + k, ranks[k], route)
        route = jnp.where(out_lane == 2 * TOP_K + k, exps[k] / denom, route)
    route_ref[...] = route


def _post(x2d, mixed, wout, bout, gffn, wr, br, *, tm):
    t, d = x2d.shape
    n_exp = wr.shape[1]
    body = functools.partial(_post_body, tm=tm, d_model=d, n_exp=n_exp)
    row_spec = pl.BlockSpec((tm, d), lambda i: (i, 0))
    return pl.pallas_call(
        body,
        grid=(t // tm,),
        in_specs=[row_spec, row_spec, _const_spec(wout.shape), _const_spec(bout.shape), _const_spec(gffn.shape),
                  _const_spec(wr.shape), _const_spec(br.shape)],
        out_specs=[row_spec, pl.BlockSpec((tm, d // 2), lambda i: (i, 0)), pl.BlockSpec((tm, LANES), lambda i: (i, 0)),
                   _const_spec((1, n_exp))],
        out_shape=[jax.ShapeDtypeStruct((t, d), F32), jax.ShapeDtypeStruct((t, d // 2), jnp.uint32),
                   jax.ShapeDtypeStruct((t, LANES), F32), jax.ShapeDtypeStruct((1, n_exp), F32)],
        scratch_shapes=[pltpu.VMEM((1, n_exp), F32)],
        compiler_params=_params(("arbitrary",)),
        name="post_router",
    )(x2d, mixed, wout, bout, gffn, wr, br)


def _gather_body(dest_ref, zlo_ref, zhi_ref, hnp_ref, xbuf_ref, stage_ref, zrow_ref, sems, zsem, *, tg, n_exp):
    s = pl.program_id(0)
    n = pl.num_programs(0)
    slot = s % 2

    def wait_step(sl):
        for _ in range(TOP_K):
            pltpu.make_async_copy(stage_ref.at[sl], xbuf_ref.at[pl.ds(0, tg)], sems.at[sl]).wait()

    @pl.when(s == 0)
    def _():
        zrow_ref[...] = jnp.zeros(zrow_ref.shape, zrow_ref.dtype)

        def pad_copy(r):
            return pltpu.make_async_copy(zrow_ref, xbuf_ref.at[pl.ds(r, 1)], zsem)

        def expert_body(e, carry):
            def start(r, c2):
                pad_copy(r).start()
                return c2

            def wait(r, c2):
                pad_copy(r).wait()
                return c2

            lax.fori_loop(zlo_ref[e], zhi_ref[e], start, 0)
            lax.fori_loop(zlo_ref[e], zhi_ref[e], wait, 0)
            return carry

        lax.fori_loop(0, n_exp, expert_body, 0)

    stage_ref[slot] = hnp_ref[...]

    def issue(j, carry):
        for k in range(TOP_K):
            dst = dest_ref[(s * tg + j) * TOP_K + k]
            pltpu.make_async_copy(stage_ref.at[slot, pl.ds(j, 1)], xbuf_ref.at[pl.ds(dst, 1)], sems.at[slot]).start()
        return carry

    lax.fori_loop(0, tg, issue, 0, unroll=DMA_UNROLL)

    @pl.when(s > 0)
    def _():
        wait_step(1 - slot)

    @pl.when(s == n - 1)
    def _():
        wait_step(slot)


def _gather(dest, zlo, zhi, hnp, n_rows):
    t, width = hnp.shape
    tg = _pick(t, (256, 128, 64))
    body = functools.partial(_gather_body, tg=tg, n_exp=zlo.shape[0])
    return pl.pallas_call(
        body,
        grid_spec=pltpu.PrefetchScalarGridSpec(
            num_scalar_prefetch=3,
            grid=(t // tg,),
            in_specs=[pl.BlockSpec((tg, width), lambda i, *_: (i, 0))],
            out_specs=pl.BlockSpec(memory_space=pl.ANY),
            scratch_shapes=[pltpu.VMEM((2, tg, width), jnp.uint32), pltpu.VMEM((1, width), jnp.uint32),
                            pltpu.SemaphoreType.DMA((2,)), pltpu.SemaphoreType.DMA(())],
        ),
        out_shape=jax.ShapeDtypeStruct((n_rows, width), jnp.uint32),
        compiler_params=_params(("arbitrary",)),
        name="dispatch_gather",
    )(dest, zlo, zhi, hnp)


def _moe_body(be_ref, nu_ref, x_ref, wgu_ref, bgu_ref, wdn_ref, bdn_ref, o_ref, *, d_ff, tf):
    blk = pl.program_id(0)

    @pl.when(blk < nu_ref[0])
    def _():
        w = x_ref[...]
        lo = pltpu.bitcast(lax.shift_left(w, jnp.uint32(16)), F32)
        hi = pltpu.bitcast(w & jnp.uint32(0xFFFF0000), F32)
        xb = jnp.concatenate([lo, hi], axis=1).astype(BF16)
        acc = None
        for j in range(d_ff // tf):
            g = jnp.dot(xb, wgu_ref[0, :, j * tf:(j + 1) * tf], preferred_element_type=F32) + bgu_ref[0, :, j * tf:(j + 1) * tf]
            u = jnp.dot(xb, wgu_ref[0, :, d_ff + j * tf:d_ff + (j + 1) * tf], preferred_element_type=F32) \
                + bgu_ref[0, :, d_ff + j * tf:d_ff + (j + 1) * tf]
            g = jnp.minimum(g, SWIGLU_LIMIT)
            u = jnp.clip(u, -SWIGLU_LIMIT, SWIGLU_LIMIT)
            act = g * _sigmoid(SWIGLU_ALPHA * g) * (u + 1.0)
            part = jnp.dot(act.astype(BF16), wdn_ref[0, j * tf:(j + 1) * tf, :], preferred_element_type=F32)
            acc = part if acc is None else acc + part
        o_ref[...] = acc + bdn_ref[0]

    @pl.when(blk >= nu_ref[0])
    def _():
        o_ref[...] = jnp.zeros(o_ref.shape, o_ref.dtype)


def _moe(block_expert, n_used, xbuf, wgu, bgu, wdn, bdn, *, n_blocks):
    n_exp, d, two_ff = wgu.shape
    d_ff = two_ff // 2
    tf = _pick(d_ff, (512, 256, 128))
    body = functools.partial(_moe_body, d_ff=d_ff, tf=tf)

    def blk_map(i, be, nu):
        return (jnp.minimum(i, nu[0] - 1), 0)

    def exp_map(i, be, nu):
        return (be[jnp.minimum(i, nu[0] - 1)], 0, 0)

    return pl.pallas_call(
        body,
        grid_spec=pltpu.PrefetchScalarGridSpec(
            num_scalar_prefetch=2,
            grid=(n_blocks,),
            in_specs=[
                pl.BlockSpec((MOE_BLOCK, d // 2), blk_map),
                pl.BlockSpec((1, d, two_ff), exp_map),
                pl.BlockSpec((1, 1, two_ff), exp_map),
                pl.BlockSpec((1, d_ff, d), exp_map, pipeline_mode=pl.Buffered(1)),
                pl.BlockSpec((1, 1, d), exp_map),
            ],
            out_specs=pl.BlockSpec((MOE_BLOCK, d), lambda i, be, nu: (i, 0)),
        ),
        out_shape=jax.ShapeDtypeStruct((n_blocks * MOE_BLOCK, d), F32),
        compiler_params=_params(("arbitrary",)),
        name="moe_ffn",
    )(block_expert, n_used, xbuf, wgu, bgu, wdn, bdn)


def _combine_body(dest_ref, y_ref, h1_ref, gates_ref, gfin_ref, o_ref, ybuf_ref, sems, *, tc):
    i = pl.program_id(0)
    n = pl.num_programs(0)

    def issue(step, slot):
        def body(j, carry):
            for k in range(TOP_K):
                d = dest_ref[(step * tc + j) * TOP_K + k]
                pltpu.make_async_copy(y_ref.at[pl.ds(d, 1)], ybuf_ref.at[slot, k, pl.ds(j, 1)], sems.at[slot]).start()
            return carry

        lax.fori_loop(0, tc, body, 0, unroll=DMA_UNROLL)

    @pl.when(i == 0)
    def _():
        issue(0, 0)

    @pl.when(i + 1 < n)
    def _():
        issue(i + 1, (i + 1) % 2)

    slot = i % 2
    for k in range(TOP_K):
        pltpu.make_async_copy(y_ref.at[pl.ds(0, tc)], ybuf_ref.at[slot, k], sems.at[slot]).wait()
    acc = h1_ref[...]
    gates = gates_ref[...]
    for k in range(TOP_K):
        acc = acc + gates[:, 2 * TOP_K + k:2 * TOP_K + k + 1] * ybuf_ref[slot, k]
    ms = jnp.mean(acc * acc, axis=-1, keepdims=True)
    o_ref[...] = acc * lax.rsqrt(ms + EPS) * gfin_ref[...]


def _combine(dest, ybuf, h1, route, gfin, *, tc):
    t, d = h1.shape
    body = functools.partial(_combine_body, tc=tc)
    return pl.pallas_call(
        body,
        grid_spec=pltpu.PrefetchScalarGridSpec(
            num_scalar_prefetch=1,
            grid=(t // tc,),
            in_specs=[
                pl.BlockSpec(memory_space=pl.ANY),
                pl.BlockSpec((tc, d), lambda i, dref: (i, 0)),
                pl.BlockSpec((tc, LANES), lambda i, dref: (i, 0)),
                pl.BlockSpec((1, d), lambda i, dref: (0, 0)),
            ],
            out_specs=pl.BlockSpec((tc, d), lambda i, dref: (i, 0)),
            scratch_shapes=[pltpu.VMEM((2, TOP_K, tc, d), F32), pltpu.SemaphoreType.DMA((2,))],
        ),
        out_shape=jax.ShapeDtypeStruct((t, d), F32),
        compiler_params=_params(("arbitrary",)),
        name="combine_norm",
    )(dest, ybuf, h1, route, gfin)


def _block_diag_gates(wa, ba, wx, bx):
    n_heads, hd, _ = wa.shape
    g = RNN_GROUP_HEADS
    n_groups = n_heads // g
    eye = jnp.eye(g, dtype=wa.dtype)

    def bd(w):
        w = w.reshape(n_groups, g, hd, hd)
        return jnp.einsum("nghj,gk->nghkj", w, eye).reshape(n_groups, g * hd, g * hd)

    wbd = jnp.concatenate([bd(wa), bd(wx)], axis=-1).astype(BF16)
    bg = jnp.concatenate([ba.reshape(n_groups, 1, g * hd), bx.reshape(n_groups, 1, g * hd)], axis=-1)
    return wbd, bg


def kernel(x, meta_tokens, norm_mix_g, w_in, b_in, conv_dw_w, conv_dw_b, conv_ln_g, conv_ln_b, w_conv_out, b_conv_out, rnn_conv_w, rnn_conv_b, rg_a_w, rg_a_b, rg_x_w, rg_x_b, rg_lambda, w_rnn_out, b_rnn_out, w_out, b_out, norm_ffn_g, w_router, b_router, w_gate_up, b_gate_up, w_down, b_down, norm_final_g):
    batch, seq, d = x.shape
    assert norm_mix_g.shape[0] == 1, "single-layer block"
    dc = conv_dw_w.shape[2]
    dr = rnn_conv_w.shape[2]
    n_exp = w_router.shape[2]
    t = batch * seq
    o1, o2, o3 = 2 * dc, 2 * dc + dr, 2 * dc + 2 * dr

    w_in0, b_in0 = w_in[0], b_in[0]
    w_c = jnp.concatenate([w_in0[:, :o1], w_in0[:, o3:o3 + d]], axis=1).astype(BF16)
    b_c = jnp.concatenate([b_in0[:o1], b_in0[o3:o3 + d]])[None]
    w_r = jnp.concatenate([w_in0[:, o1:o3], w_in0[:, o3 + d:]], axis=1).astype(BF16)
    b_r = jnp.concatenate([b_in0[o1:o3], b_in0[o3 + d:]])[None]
    wbd, bg = _block_diag_gates(rg_a_w[0], rg_a_b[0], rg_x_w[0], rg_x_b[0])
    g_mix = norm_mix_g[0][None]

    x2d = x.reshape(t, d)
    tm_in = _pick(t, (1024, 512, 256, 128))
    tm = _pick(seq, (256, 128))

    pc = _inproj(x2d, g_mix, w_c, b_c, tm_in)
    pr = _inproj(x2d, g_mix, w_r, b_r, tm_in)
    pcm = _inproj(meta_tokens, g_mix, w_c, b_c, meta_tokens.shape[0])
    prm = _inproj(meta_tokens, g_mix, w_r, b_r, meta_tokens.shape[0])

    mc = _conv_branch(pc, pcm, conv_dw_w[0], conv_dw_b[0][None], conv_ln_g[0][None], conv_ln_b[0][None],
                      w_conv_out[0].astype(BF16), b_conv_out[0][None], batch=batch, seq=seq, tm=tm, d_model=d)
    mixed = _rnn_branch(pr, prm, mc, rnn_conv_w[0], rnn_conv_b[0][None], wbd, bg, rg_lambda[0][None],
                        w_rnn_out[0].astype(BF16), b_rnn_out[0][None], batch=batch, seq=seq, tm=tm, d_model=d)
    wr = jnp.pad(w_router[0], ((0, 0), (0, LANES - n_exp)))
    br = jnp.pad(b_router[0], (0, LANES - n_exp), constant_values=-jnp.inf)[None]
    h1, hnp, route, counts = _post(x2d, mixed, w_out[0].astype(BF16), b_out[0][None], norm_ffn_g[0][None],
                                   wr, br, tm=tm)

    counts = counts[0, :n_exp].astype(jnp.int32)
    padded = (counts + MOE_BLOCK - 1) // MOE_BLOCK * MOE_BLOCK
    pad_end = jnp.cumsum(padded)
    pad_start = pad_end - padded
    n_blocks = (t * TOP_K + n_exp * (MOE_BLOCK - 1)) // MOE_BLOCK
    n_used = (pad_end[-1] // MOE_BLOCK).astype(jnp.int32)[None]
    block_expert = jnp.minimum(
        jnp.sum((jnp.arange(n_blocks, dtype=jnp.int32)[:, None] * MOE_BLOCK >= pad_end[None, :]).astype(jnp.int32), axis=1),
        n_exp - 1).astype(jnp.int32)
    idx = route[:, 0:TOP_K].astype(jnp.int32)
    rank = route[:, TOP_K:2 * TOP_K].astype(jnp.int32)
    start_of = jnp.sum(jnp.where(idx[..., None] == jnp.arange(n_exp, dtype=jnp.int32), pad_start, 0), axis=-1)
    dest = (start_of + rank).reshape(t * TOP_K)

    n_rows = n_blocks * MOE_BLOCK
    zero_lo = jnp.concatenate([pad_start + counts, pad_end[-1:]]).astype(jnp.int32)
    zero_hi = jnp.concatenate([pad_end, jnp.full((1,), n_rows, jnp.int32)]).astype(jnp.int32)
    xbuf = _gather(dest, zero_lo, zero_hi, hnp, n_rows)
    ybuf = _moe(block_expert, n_used, xbuf, w_gate_up[0].astype(BF16), b_gate_up[0][:, None, :],
                w_down[0].astype(BF16), b_down[0][:, None, :], n_blocks=n_blocks)
    out = _combine(dest, ybuf, h1, route, norm_final_g[None], tc=_pick(t, (128, 64)))
    return out.reshape(batch, seq, d)
```

```python
import functools

import jax
import jax.numpy as jnp
from jax import lax
from jax.experimental import pallas as pl
from jax.experimental.pallas import tpu as pltpu

F32 = jnp.float32
BF16 = jnp.bfloat16

EPS = 1e-6
RG_C = 8.0
TOP_K = 4
SWIGLU_ALPHA = 1.702
SWIGLU_LIMIT = 7.0
MOE_BLOCK = 256
RNN_GROUP_HEADS = 4
CONV_HIST = 32
CONV_STRIDE = 4
RNN_HIST = 8
DMA_UNROLL = 4
LANES = 128
SUBLANES = 8
VMEM_LIMIT = 56 * 1024 * 1024
INPROJ_VMEM_LIMIT = 60 * 1024 * 1024
CAST_BLOCK_ELEMS = 1 << 20


def _params(sem, vmem=VMEM_LIMIT):
    return pltpu.CompilerParams(dimension_semantics=sem, vmem_limit_bytes=vmem)


def _pick(n, cands):
    for c in cands:
        if n % c == 0:
            return c
    return n


def _const_spec(shape):
    zeros = (0,) * len(shape)
    return pl.BlockSpec(shape, lambda *_: zeros)


def _sigmoid(x):
    return 0.5 * jnp.tanh(0.5 * x) + 0.5


def _inproj_body(x_ref, g_ref, w_ref, b_ref, o_ref, xn_ref):
    @pl.when(pl.program_id(1) == 0)
    def _():
        x = x_ref[...]
        ms = jnp.mean(x * x, axis=-1, keepdims=True)
        xn_ref[...] = (x * lax.rsqrt(ms + EPS) * g_ref[...]).astype(BF16)

    acc = jnp.dot(xn_ref[...], w_ref[...], preferred_element_type=F32)
    o_ref[...] = (acc + b_ref[...]).astype(o_ref.dtype)


def _inproj(x2d, g, w, b, tm):
    m, k = x2d.shape
    n = w.shape[1]
    tn = _pick(n, (1024, 512, 256, 128))
    return pl.pallas_call(
        _inproj_body,
        grid=(m // tm, n // tn),
        in_specs=[
            pl.BlockSpec((tm, k), lambda i, j: (i, 0)),
            pl.BlockSpec((1, k), lambda i, j: (0, 0)),
            pl.BlockSpec((k, tn), lambda i, j: (0, j)),
            pl.BlockSpec((1, tn), lambda i, j: (0, j)),
        ],
        out_specs=pl.BlockSpec((tm, tn), lambda i, j: (i, j)),
        out_shape=jax.ShapeDtypeStruct((m, n), BF16),
        scratch_shapes=[pltpu.VMEM((tm, k), BF16)],
        compiler_params=_params(("arbitrary", "arbitrary")),
        name="inproj",
    )(x2d, g, w, b)


def _inproj_cast_body(x_ref, g_ref, w_ref, b_ref, wgu_ref, wdn_ref, oc_ref, or_ref, wgu_o_ref, wdn_o_ref, xn_ref,
                      *, nj_c, n_gu, n_dn):
    i = pl.program_id(0)
    j = pl.program_id(1)

    @pl.when(j == 0)
    def _():
        x = x_ref[...]
        ms = jnp.mean(x * x, axis=-1, keepdims=True)
        xn_ref[...] = (x * lax.rsqrt(ms + EPS) * g_ref[...]).astype(BF16)

    acc = (jnp.dot(xn_ref[...], w_ref[...], preferred_element_type=F32) + b_ref[...]).astype(BF16)

    @pl.when(j < nj_c)
    def _():
        oc_ref[...] = acc

    @pl.when(j >= nj_c)
    def _():
        or_ref[...] = acc

    s = i * pl.num_programs(1) + j

    @pl.when(s < n_gu)
    def _():
        wgu_o_ref[...] = wgu_ref[...].astype(BF16)

    @pl.when(jnp.logical_and(s >= n_gu, s < n_gu + n_dn))
    def _():
        wdn_o_ref[...] = wdn_ref[...].astype(BF16)


def _inproj_cast(x2d, g, w, b, wgu, wdn, *, n_c, tm):
    m, k = x2d.shape
    n = w.shape[1]
    tn = _pick(n_c, (512, 256, 128))
    assert (n - n_c) % tn == 0
    nj_c, nj = n_c // tn, n // tn
    n_exp, d, two_ff = wgu.shape
    d_ff = wdn.shape[1]
    rows_gu = min(d, max(SUBLANES, CAST_BLOCK_ELEMS // two_ff))
    rows_dn = min(d_ff, max(SUBLANES, CAST_BLOCK_ELEMS // d))
    per_gu, per_dn = d // rows_gu, d_ff // rows_dn
    n_gu, n_dn = n_exp * per_gu, n_exp * per_dn
    assert (m // tm) * nj >= n_gu + n_dn, "not enough grid steps to convert the expert weights"

    def gu_map(i, j):
        blk = jnp.minimum(i * nj + j, n_gu - 1)
        return (blk // per_gu, blk % per_gu, 0)

    def dn_map(i, j):
        blk = jnp.clip(i * nj + j - n_gu, 0, n_dn - 1)
        return (blk // per_dn, blk % per_dn, 0)

    body = functools.partial(_inproj_cast_body, nj_c=nj_c, n_gu=n_gu, n_dn=n_dn)
    return pl.pallas_call(
        body,
        grid=(m // tm, nj),
        in_specs=[
            pl.BlockSpec((tm, k), lambda i, j: (i, 0)),
            pl.BlockSpec((1, k), lambda i, j: (0, 0)),
            pl.BlockSpec((k, tn), lambda i, j: (0, j)),
            pl.BlockSpec((1, tn), lambda i, j: (0, j)),
            pl.BlockSpec((1, rows_gu, two_ff), gu_map),
            pl.BlockSpec((1, rows_dn, d), dn_map),
        ],
        out_specs=[
            pl.BlockSpec((tm, tn), lambda i, j: (i, jnp.minimum(j, nj_c - 1))),
            pl.BlockSpec((tm, tn), lambda i, j: (i, jnp.maximum(j - nj_c, 0))),
            pl.BlockSpec((1, rows_gu, two_ff), gu_map),
            pl.BlockSpec((1, rows_dn, d), dn_map),
        ],
        out_shape=[
            jax.ShapeDtypeStruct((m, n_c), BF16),
            jax.ShapeDtypeStruct((m, n - n_c), BF16),
            jax.ShapeDtypeStruct(wgu.shape, BF16),
            jax.ShapeDtypeStruct(wdn.shape, BF16),
        ],
        scratch_shapes=[pltpu.VMEM((tm, k), BF16)],
        compiler_params=_params(("arbitrary", "arbitrary"), vmem=INPROJ_VMEM_LIMIT),
        name="inproj_cast",
    )(x2d, g, w, b, wgu, wdn)


def _conv_body(pc_ref, pcm_ref, dww_ref, dwb_ref, lng_ref, lnb_ref, wpw_ref, bpw_ref, o_ref,
               ext_ref, cv_ref, yb_ref, *, tm, dc, width, n_meta):
    i = pl.program_id(1)
    nc = dc // LANES

    def cols(c):
        return slice(c * LANES, (c + 1) * LANES)

    @pl.when(i == 0)
    def _():
        for c in range(nc):
            am = pcm_ref[:, cols(c)].astype(F32)
            gm = pcm_ref[:, dc + c * LANES:dc + (c + 1) * LANES].astype(F32)
            ext_ref[c, 0:CONV_HIST - n_meta, :] = jnp.zeros((CONV_HIST - n_meta, LANES), F32)
            ext_ref[c, CONV_HIST - n_meta:CONV_HIST, :] = am * _sigmoid(gm)

    @pl.when(i > 0)
    def _():
        ext_ref[:, 0:CONV_HIST, :] = ext_ref[:, tm:tm + CONV_HIST, :]

    for c in range(nc):
        a = pc_ref[:, cols(c)].astype(F32)
        g = pc_ref[:, dc + c * LANES:dc + (c + 1) * LANES].astype(F32)
        ext_ref[c, CONV_HIST:CONV_HIST + tm, :] = a * _sigmoid(g)

    base = CONV_HIST - (width - 1)
    sub = SUBLANES * CONV_STRIDE

    def col_body(c, carry):
        taps = [jnp.broadcast_to(dww_ref[c, k:k + 1, :], (SUBLANES, LANES)) for k in range(width)]
        for j in range(tm // sub):
            for q in range(CONV_STRIDE):
                acc = None
                for k in range(width):
                    xv = ext_ref[c, pl.ds(j * sub + q + base + k, SUBLANES, stride=CONV_STRIDE), :]
                    acc = taps[k] * xv if acc is None else acc + taps[k] * xv
                cv_ref[c, pl.ds(j * sub + q, SUBLANES, stride=CONV_STRIDE), :] = acc
        return carry

    lax.fori_loop(0, nc, col_body, 0)

    total = None
    for c in range(nc):
        vc = cv_ref[c] + dwb_ref[:, cols(c)]
        cv_ref[c] = vc
        total = vc if total is None else total + vc
    mu = jnp.sum(total, axis=-1, keepdims=True) * (1.0 / dc)
    sq = None
    for c in range(nc):
        dcen = cv_ref[c] - mu
        sq = dcen * dcen if sq is None else sq + dcen * dcen
    rstd = lax.rsqrt(jnp.sum(sq, axis=-1, keepdims=True) * (1.0 / dc) + EPS)
    for c in range(nc):
        y = (cv_ref[c] - mu) * rstd * lng_ref[:, cols(c)] + lnb_ref[:, cols(c)]
        yb_ref[:, cols(c)] = (y * _sigmoid(y)).astype(BF16)
    z = jnp.dot(yb_ref[...], wpw_ref[...], preferred_element_type=F32) + bpw_ref[...]
    gate = _sigmoid(pc_ref[:, 2 * dc:].astype(F32))
    o_ref[...] = gate * z


def _conv_branch(pc, pcm, dww, dwb, lng, lnb, wpw, bpw, *, batch, seq, tm, d_model):
    width, dc = dww.shape
    nc = dc // LANES
    n_meta = pcm.shape[0]
    nt = seq // tm
    wpad = -(-width // SUBLANES) * SUBLANES
    dww = jnp.pad(dww, ((0, wpad - width), (0, 0))).reshape(wpad, nc, LANES).transpose(1, 0, 2)
    body = functools.partial(_conv_body, tm=tm, dc=dc, width=width, n_meta=n_meta)
    return pl.pallas_call(
        body,
        grid=(batch, nt),
        in_specs=[
            pl.BlockSpec((tm, pc.shape[1]), lambda b, i: (b * nt + i, 0)),
            _const_spec(pcm.shape),
            _const_spec(dww.shape),
            _const_spec(dwb.shape),
            _const_spec(lng.shape),
            _const_spec(lnb.shape),
            _const_spec(wpw.shape),
            _const_spec(bpw.shape),
        ],
        out_specs=pl.BlockSpec((tm, d_model), lambda b, i: (b * nt + i, 0)),
        out_shape=jax.ShapeDtypeStruct((batch * seq, d_model), F32),
        scratch_shapes=[pltpu.VMEM((nc, tm + CONV_HIST, LANES), F32), pltpu.VMEM((nc, tm, LANES), F32),
                        pltpu.VMEM((tm, dc), BF16)],
        compiler_params=_params(("arbitrary", "arbitrary")),
        name="conv_branch",
    )(pc, pcm, dww, dwb, lng, lnb, wpw, bpw)


def _gelu_tanh(x):
    return 0.5 * x * (1.0 + jnp.tanh(0.7978845608028654 * (x + 0.044715 * (x * x * x))))


def _softplus(x):
    return jnp.maximum(x, 0.0) + jnp.log(1.0 + jnp.exp(-jnp.abs(x)))


def _rnn_body(pr_ref, prm_ref, mc_ref, cw_ref, cb_ref, wbd_ref, bg_ref, lam_ref, wo_ref, bo_ref, o_ref,
              ext_ref, hc_ref, a_ref, b_ref, h_ref, *, tm, dr, width, n_meta, gw):
    i = pl.program_id(1)
    n_groups = dr // gw
    c_decay = -RG_C * _softplus(-lam_ref[...])
    row = lax.broadcasted_iota(jnp.int32, (SUBLANES, dr), 0)

    def recurrence(n):
        base = RNN_HIST - (width - 1)
        xc = cb_ref[...] + cw_ref[0:1, :] * ext_ref[base:base + n, :]
        for k in range(1, width):
            xc = xc + cw_ref[k:k + 1, :] * ext_ref[base + k:base + k + n, :]
        xcb = xc.astype(BF16)
        for gi in range(n_groups):
            lo, hi = gi * gw, (gi + 1) * gw
            z = jnp.dot(xcb[:, lo:hi], wbd_ref[gi], preferred_element_type=F32) + bg_ref[gi]
            r = _sigmoid(z[:, 0:gw])
            ig = _sigmoid(z[:, gw:2 * gw])
            a = jnp.exp(c_decay[:, lo:hi] * r)
            mult = jnp.sqrt(jnp.maximum(1.0 - a * a, 0.0))
            a_ref[0:n, lo:hi] = a
            b_ref[0:n, lo:hi] = mult * (ig * xc[:, lo:hi])

        def scan_body(s, carry):
            r0 = pl.multiple_of(s * SUBLANES, SUBLANES)
            av = a_ref[pl.ds(r0, SUBLANES), :]
            bv = b_ref[pl.ds(r0, SUBLANES), :]
            for d in (1, 2, 4):
                keep = row >= d
                a_s = pltpu.roll(av, d, axis=0)
                b_s = pltpu.roll(bv, d, axis=0)
                bv = jnp.where(keep, av * b_s + bv, bv)
                av = jnp.where(keep, av * a_s, av)
            h = av * carry + bv
            h_ref[pl.ds(r0, SUBLANES), :] = h
            return jnp.broadcast_to(h[SUBLANES - 1:SUBLANES, :], (SUBLANES, dr))

        hc_ref[...] = lax.fori_loop(0, n // SUBLANES, scan_body, hc_ref[...])

    @pl.when(i == 0)
    def _():
        ext_ref[0:RNN_HIST, :] = jnp.zeros((RNN_HIST, dr), F32)
        ext_ref[RNN_HIST:RNN_HIST + n_meta, :] = prm_ref[:, 0:dr].astype(F32)
        hc_ref[...] = jnp.zeros((SUBLANES, dr), F32)
        recurrence(n_meta)
        ext_ref[0:RNN_HIST, :] = ext_ref[n_meta:n_meta + RNN_HIST, :]

    @pl.when(i > 0)
    def _():
        ext_ref[0:RNN_HIST, :] = ext_ref[tm:tm + RNN_HIST, :]

    ext_ref[RNN_HIST:RNN_HIST + tm, :] = pr_ref[:, 0:dr].astype(F32)
    recurrence(tm)
    y = h_ref[...] * _gelu_tanh(pr_ref[:, dr:2 * dr].astype(F32))
    z = jnp.dot(y.astype(BF16), wo_ref[...], preferred_element_type=F32) + bo_ref[...]
    gate = _sigmoid(pr_ref[:, 2 * dr:].astype(F32))
    o_ref[...] = mc_ref[...] + gate * z


def _rnn_branch(pr, prm, mc, cw, cb, wbd, bg, lam, wo, bo, *, batch, seq, tm, d_model):
    dr = cw.shape[1]
    width = cw.shape[0]
    n_meta = prm.shape[0]
    gw = wbd.shape[1]
    nt = seq // tm
    body = functools.partial(_rnn_body, tm=tm, dr=dr, width=width, n_meta=n_meta, gw=gw)
    return pl.pallas_call(
        body,
        grid=(batch, nt),
        in_specs=[
            pl.BlockSpec((tm, pr.shape[1]), lambda b, i: (b * nt + i, 0)),
            _const_spec(prm.shape),
            pl.BlockSpec((tm, d_model), lambda b, i: (b * nt + i, 0)),
            _const_spec(cw.shape),
            _const_spec(cb.shape),
            _const_spec(wbd.shape),
            _const_spec(bg.shape),
            _const_spec(lam.shape),
            _const_spec(wo.shape),
            _const_spec(bo.shape),
        ],
        out_specs=pl.BlockSpec((tm, d_model), lambda b, i: (b * nt + i, 0)),
        out_shape=jax.ShapeDtypeStruct((batch * seq, d_model), F32),
        scratch_shapes=[
            pltpu.VMEM((tm + RNN_HIST, dr), F32),
            pltpu.VMEM((SUBLANES, dr), F32),
            pltpu.VMEM((tm, dr), F32),
            pltpu.VMEM((tm, dr), F32),
            pltpu.VMEM((tm, dr), F32),
        ],
        compiler_params=_params(("arbitrary", "arbitrary")),
        name="rnn_branch",
    )(pr, prm, mc, cw, cb, wbd, bg, lam, wo, bo)


def _split_bf16(v):
    hi = v.astype(BF16)
    lo = (v - hi.astype(F32)).astype(BF16)
    return hi, lo


def _post_body(x_ref, mix_ref, wout_ref, bout_ref, gffn_ref, wr_ref, br_ref,
               h1_ref, hnp_ref, route_ref, cnt_ref, run_ref, *, tm, d_model, n_exp):
    step = pl.program_id(0)

    @pl.when(step == 0)
    def _():
        run_ref[...] = jnp.zeros((1, n_exp), F32)

    h1 = x_ref[...] + jnp.dot(mix_ref[...].astype(BF16), wout_ref[...], preferred_element_type=F32) + bout_ref[...]
    h1_ref[...] = h1
    ms = jnp.mean(h1 * h1, axis=-1, keepdims=True)
    hn = h1 * lax.rsqrt(ms + EPS) * gffn_ref[...]

    half = d_model // 2
    hb = hn.astype(BF16).astype(F32)
    lo_bits = lax.shift_right_logical(pltpu.bitcast(hb[:, 0:half], jnp.uint32), jnp.uint32(16))
    hi_bits = pltpu.bitcast(hb[:, half:], jnp.uint32) & jnp.uint32(0xFFFF0000)
    hnp_ref[...] = lo_bits | hi_bits

    hn_hi, hn_lo = _split_bf16(hn)
    w_hi, w_lo = _split_bf16(wr_ref[...])
    logits = (jnp.dot(hn_hi, w_hi, preferred_element_type=F32)
              + jnp.dot(hn_hi, w_lo, preferred_element_type=F32)
              + jnp.dot(hn_lo, w_hi, preferred_element_type=F32)) + br_ref[...]

    lane = lax.broadcasted_iota(jnp.int32, (tm, n_exp), 1).astype(F32)
    work = logits
    vals, idxs = [], []
    for _ in range(TOP_K):
        m = jnp.max(work, axis=-1, keepdims=True)
        ik = jnp.min(jnp.where(work == m, lane, float(n_exp)), axis=-1, keepdims=True)
        vals.append(m)
        idxs.append(ik)
        work = jnp.where(lane == ik, -jnp.inf, work)
    exps = [jnp.exp(v - vals[0]) for v in vals]
    denom = exps[0] + exps[1] + exps[2] + exps[3]

    onehots = [(lane == ik).astype(F32) for ik in idxs]
    sel = onehots[0] + onehots[1] + onehots[2] + onehots[3]
    r_i = lax.broadcasted_iota(jnp.int32, (tm, tm), 0)
    c_i = lax.broadcasted_iota(jnp.int32, (tm, tm), 1)
    strict_lower = (c_i < r_i).astype(BF16)
    before = jnp.dot(strict_lower, sel.astype(BF16), preferred_element_type=F32) + run_ref[...]
    ranks = [jnp.sum(before * oh, axis=-1, keepdims=True) for oh in onehots]
    run_ref[...] = run_ref[...] + jnp.sum(sel, axis=0, keepdims=True)
    cnt_ref[...] = run_ref[...]

    out_lane = lax.broadcasted_iota(jnp.int32, (tm, LANES), 1)
    route = jnp.zeros((tm, LANES), F32)
    for k in range(TOP_K):
        route = jnp.where(out_lane == k, idxs[k], route)
        route = jnp.where(out_lane == TOP_K + k, ranks[k], route)
        route = jnp.where(out_lane == 2 * TOP_K + k, exps[k] / denom, route)
    route_ref[...] = route


def _post(x2d, mixed, wout, bout, gffn, wr, br, *, tm):
    t, d = x2d.shape
    n_exp = wr.shape[1]
    body = functools.partial(_post_body, tm=tm, d_model=d, n_exp=n_exp)
    row_spec = pl.BlockSpec((tm, d), lambda i: (i, 0))
    return pl.pallas_call(
        body,
        grid=(t // tm,),
        in_specs=[row_spec, row_spec, _const_spec(wout.shape), _const_spec(bout.shape), _const_spec(gffn.shape),
                  _const_spec(wr.shape), _const_spec(br.shape)],
        out_specs=[row_spec, pl.BlockSpec((tm, d // 2), lambda i: (i, 0)), pl.BlockSpec((tm, LANES), lambda i: (i, 0)),
                   _const_spec((1, n_exp))],
        out_shape=[jax.ShapeDtypeStruct((t, d), F32), jax.ShapeDtypeStruct((t, d // 2), jnp.uint32),
                   jax.ShapeDtypeStruct((t, LANES), F32), jax.ShapeDtypeStruct((1, n_exp), F32)],
        scratch_shapes=[pltpu.VMEM((1, n_exp), F32)],
        compiler_params=_params(("arbitrary",)),
        name="post_router",
    )(x2d, mixed, wout, bout, gffn, wr, br)


def _gather_body(dest_ref, zlo_ref, zhi_ref, hnp_ref, xbuf_ref, stage_ref, zrow_ref, sems, zsem, *, tg, n_exp):
    s = pl.program_id(0)
    n = pl.num_programs(0)
    slot = s % 2

    def wait_step(sl):
        for _ in range(TOP_K):
            pltpu.make_async_copy(stage_ref.at[sl], xbuf_ref.at[pl.ds(0, tg)], sems.at[sl]).wait()

    @pl.when(s == 0)
    def _():
        zrow_ref[...] = jnp.zeros(zrow_ref.shape, zrow_ref.dtype)

        def pad_copy(r):
            return pltpu.make_async_copy(zrow_ref, xbuf_ref.at[pl.ds(r, 1)], zsem)

        def expert_body(e, carry):
            def start(r, c2):
                pad_copy(r).start()
                return c2

            def wait(r, c2):
                pad_copy(r).wait()
                return c2

            lax.fori_loop(zlo_ref[e], zhi_ref[e], start, 0)
            lax.fori_loop(zlo_ref[e], zhi_ref[e], wait, 0)
            return carry

        lax.fori_loop(0, n_exp, expert_body, 0)

    stage_ref[slot] = hnp_ref[...]

    def issue(j, carry):
        for k in range(TOP_K):
            dst = dest_ref[(s * tg + j) * TOP_K + k]
            pltpu.make_async_copy(stage_ref.at[slot, pl.ds(j, 1)], xbuf_ref.at[pl.ds(dst, 1)], sems.at[slot]).start()
        return carry

    lax.fori_loop(0, tg, issue, 0, unroll=DMA_UNROLL)

    @pl.when(s > 0)
    def _():
        wait_step(1 - slot)

    @pl.when(s == n - 1)
    def _():
        wait_step(slot)


def _gather(dest, zlo, zhi, hnp, n_rows):
    t, width = hnp.shape
    tg = _pick(t, (256, 128, 64))
    body = functools.partial(_gather_body, tg=tg, n_exp=zlo.shape[0])
    return pl.pallas_call(
        body,
        grid_spec=pltpu.PrefetchScalarGridSpec(
            num_scalar_prefetch=3,
            grid=(t // tg,),
            in_specs=[pl.BlockSpec((tg, width), lambda i, *_: (i, 0))],
            out_specs=pl.BlockSpec(memory_space=pl.ANY),
            scratch_shapes=[pltpu.VMEM((2, tg, width), jnp.uint32), pltpu.VMEM((1, width), jnp.uint32),
                            pltpu.SemaphoreType.DMA((2,)), pltpu.SemaphoreType.DMA(())],
        ),
        out_shape=jax.ShapeDtypeStruct((n_rows, width), jnp.uint32),
        compiler_params=_params(("arbitrary",)),
        name="dispatch_gather",
    )(dest, zlo, zhi, hnp)


def _moe_body(be_ref, nu_ref, x_ref, wgu_ref, bgu_ref, wdn_ref, bdn_ref, o_ref, *, d_ff, tf):
    blk = pl.program_id(0)

    @pl.when(blk < nu_ref[0])
    def _():
        w = x_ref[...]
        lo = pltpu.bitcast(lax.shift_left(w, jnp.uint32(16)), F32)
        hi = pltpu.bitcast(w & jnp.uint32(0xFFFF0000), F32)
        xb = jnp.concatenate([lo, hi], axis=1).astype(BF16)
        acc = None
        for j in range(d_ff // tf):
            g = jnp.dot(xb, wgu_ref[0, :, j * tf:(j + 1) * tf], preferred_element_type=F32) + bgu_ref[0, :, j * tf:(j + 1) * tf]
            u = jnp.dot(xb, wgu_ref[0, :, d_ff + j * tf:d_ff + (j + 1) * tf], preferred_element_type=F32) \
                + bgu_ref[0, :, d_ff + j * tf:d_ff + (j + 1) * tf]
            g = jnp.minimum(g, SWIGLU_LIMIT)
            u = jnp.clip(u, -SWIGLU_LIMIT, SWIGLU_LIMIT)
            act = g * _sigmoid(SWIGLU_ALPHA * g) * (u + 1.0)
            part = jnp.dot(act.astype(BF16), wdn_ref[0, j * tf:(j + 1) * tf, :], preferred_element_type=F32)
            acc = part if acc is None else acc + part
        o_ref[...] = acc + bdn_ref[0]

    @pl.when(blk >= nu_ref[0])
    def _():
        o_ref[...] = jnp.zeros(o_ref.shape, o_ref.dtype)


def _moe(block_expert, n_used, xbuf, wgu, bgu, wdn, bdn, *, n_blocks):
    n_exp, d, two_ff = wgu.shape
    d_ff = two_ff // 2
    tf = _pick(d_ff, (512, 256, 128))
    body = functools.partial(_moe_body, d_ff=d_ff, tf=tf)

    def blk_map(i, be, nu):
        return (jnp.minimum(i, nu[0] - 1), 0)

    def exp_map(i, be, nu):
        return (be[jnp.minimum(i, nu[0] - 1)], 0, 0)

    return pl.pallas_call(
        body,
        grid_spec=pltpu.PrefetchScalarGridSpec(
            num_scalar_prefetch=2,
            grid=(n_blocks,),
            in_specs=[
                pl.BlockSpec((MOE_BLOCK, d // 2), blk_map),
                pl.BlockSpec((1, d, two_ff), exp_map),
                pl.BlockSpec((1, 1, two_ff), exp_map),
                pl.BlockSpec((1, d_ff, d), exp_map, pipeline_mode=pl.Buffered(1)),
                pl.BlockSpec((1, 1, d), exp_map),
            ],
            out_specs=pl.BlockSpec((MOE_BLOCK, d), lambda i, be, nu: (i, 0)),
        ),
        out_shape=jax.ShapeDtypeStruct((n_blocks * MOE_BLOCK, d), F32),
        compiler_params=_params(("arbitrary",)),
        name="moe_ffn",
    )(block_expert, n_used, xbuf, wgu, bgu, wdn, bdn)


def _combine_body(dest_ref, y_ref, h1_ref, gates_ref, gfin_ref, o_ref, ybuf_ref, sems, *, tc):
    i = pl.program_id(0)
    n = pl.num_programs(0)

    def issue(step, slot):
        def body(j, carry):
            for k in range(TOP_K):
                d = dest_ref[(step * tc + j) * TOP_K + k]
                pltpu.make_async_copy(y_ref.at[pl.ds(d, 1)], ybuf_ref.at[slot, k, pl.ds(j, 1)], sems.at[slot]).start()
            return carry

        lax.fori_loop(0, tc, body, 0, unroll=DMA_UNROLL)

    @pl.when(i == 0)
    def _():
        issue(0, 0)

    @pl.when(i + 1 < n)
    def _():
        issue(i + 1, (i + 1) % 2)

    slot = i % 2
    for k in range(TOP_K):
        pltpu.make_async_copy(y_ref.at[pl.ds(0, tc)], ybuf_ref.at[slot, k], sems.at[slot]).wait()
    acc = h1_ref[...]
    gates = gates_ref[...]
    for k in range(TOP_K):
        acc = acc + gates[:, 2 * TOP_K + k:2 * TOP_K + k + 1] * ybuf_ref[slot, k]
    ms = jnp.mean(acc * acc, axis=-1, keepdims=True)
    o_ref[...] = acc * lax.rsqrt(ms + EPS) * gfin_ref[...]


def _combine(dest, ybuf, h1, route, gfin, *, tc):
    t, d = h1.shape
    body = functools.partial(_combine_body, tc=tc)
    return pl.pallas_call(
        body,
        grid_spec=pltpu.PrefetchScalarGridSpec(
            num_scalar_prefetch=1,
            grid=(t // tc,),
            in_specs=[
                pl.BlockSpec(memory_space=pl.ANY),
                pl.BlockSpec((tc, d), lambda i, dref: (i, 0)),
                pl.BlockSpec((tc, LANES), lambda i, dref: (i, 0)),
                pl.BlockSpec((1, d), lambda i, dref: (0, 0)),
            ],
            out_specs=pl.BlockSpec((tc, d), lambda i, dref: (i, 0)),
            scratch_shapes=[pltpu.VMEM((2, TOP_K, tc, d), F32), pltpu.SemaphoreType.DMA((2,))],
        ),
        out_shape=jax.ShapeDtypeStruct((t, d), F32),
        compiler_params=_params(("arbitrary",)),
        name="combine_norm",
    )(dest, ybuf, h1, route, gfin)


def _block_diag_gates(wa, ba, wx, bx):
    n_heads, hd, _ = wa.shape
    g = RNN_GROUP_HEADS
    n_groups = n_heads // g
    eye = jnp.eye(g, dtype=wa.dtype)

    def bd(w):
        w = w.reshape(n_groups, g, hd, hd)
        return jnp.einsum("nghj,gk->nghkj", w, eye).reshape(n_groups, g * hd, g * hd)

    wbd = jnp.concatenate([bd(wa), bd(wx)], axis=-1).astype(BF16)
    bg = jnp.concatenate([ba.reshape(n_groups, 1, g * hd), bx.reshape(n_groups, 1, g * hd)], axis=-1)
    return wbd, bg


def kernel(x, meta_tokens, norm_mix_g, w_in, b_in, conv_dw_w, conv_dw_b, conv_ln_g, conv_ln_b, w_conv_out, b_conv_out, rnn_conv_w, rnn_conv_b, rg_a_w, rg_a_b, rg_x_w, rg_x_b, rg_lambda, w_rnn_out, b_rnn_out, w_out, b_out, norm_ffn_g, w_router, b_router, w_gate_up, b_gate_up, w_down, b_down, norm_final_g):
    batch, seq, d = x.shape
    assert norm_mix_g.shape[0] == 1, "single-layer block"
    dc = conv_dw_w.shape[2]
    dr = rnn_conv_w.shape[2]
    n_exp = w_router.shape[2]
    t = batch * seq
    o1, o2, o3 = 2 * dc, 2 * dc + dr, 2 * dc + 2 * dr

    w_in0, b_in0 = w_in[0], b_in[0]
    n_c = o1 + d
    w_all = jnp.concatenate([w_in0[:, :o1], w_in0[:, o3:o3 + d], w_in0[:, o1:o3], w_in0[:, o3 + d:]], axis=1).astype(BF16)
    b_all = jnp.concatenate([b_in0[:o1], b_in0[o3:o3 + d], b_in0[o1:o3], b_in0[o3 + d:]])[None]
    wbd, bg = _block_diag_gates(rg_a_w[0], rg_a_b[0], rg_x_w[0], rg_x_b[0])
    g_mix = norm_mix_g[0][None]

    x2d = x.reshape(t, d)
    tm_in = _pick(t, (1024, 512, 256, 128))
    tm = _pick(seq, (256, 128))

    pc, pr, wgu_bf16, wdn_bf16 = _inproj_cast(x2d, g_mix, w_all, b_all, w_gate_up[0], w_down[0], n_c=n_c, tm=tm_in)
    pm = _inproj(meta_tokens, g_mix, w_all, b_all, meta_tokens.shape[0])
    pcm, prm = pm[:, :n_c], pm[:, n_c:]

    mc = _conv_branch(pc, pcm, conv_dw_w[0], conv_dw_b[0][None], conv_ln_g[0][None], conv_ln_b[0][None],
                      w_conv_out[0].astype(BF16), b_conv_out[0][None], batch=batch, seq=seq, tm=tm, d_model=d)
    mixed = _rnn_branch(pr, prm, mc, rnn_conv_w[0], rnn_conv_b[0][None], wbd, bg, rg_lambda[0][None],
                        w_rnn_out[0].astype(BF16), b_rnn_out[0][None], batch=batch, seq=seq, tm=tm, d_model=d)
    wr = jnp.pad(w_router[0], ((0, 0), (0, LANES - n_exp)))
    br = jnp.pad(b_router[0], (0, LANES - n_exp), constant_values=-jnp.inf)[None]
    h1, hnp, route, counts = _post(x2d, mixed, w_out[0].astype(BF16), b_out[0][None], norm_ffn_g[0][None],
                                   wr, br, tm=tm)

    counts = counts[0, :n_exp].astype(jnp.int32)
    padded = (counts + MOE_BLOCK - 1) // MOE_BLOCK * MOE_BLOCK
    pad_end = jnp.cumsum(padded)
    pad_start = pad_end - padded
    n_blocks = (t * TOP_K + n_exp * (MOE_BLOCK - 1)) // MOE_BLOCK
    n_used = (pad_end[-1] // MOE_BLOCK).astype(jnp.int32)[None]
    block_expert = jnp.minimum(
        jnp.sum((jnp.arange(n_blocks, dtype=jnp.int32)[:, None] * MOE_BLOCK >= pad_end[None, :]).astype(jnp.int32), axis=1),
        n_exp - 1).astype(jnp.int32)
    idx = route[:, 0:TOP_K].astype(jnp.int32)
    rank = route[:, TOP_K:2 * TOP_K].astype(jnp.int32)
    start_of = jnp.sum(jnp.where(idx[..., None] == jnp.arange(n_exp, dtype=jnp.int32), pad_start, 0), axis=-1)
    dest = (start_of + rank).reshape(t * TOP_K)

    n_rows = n_blocks * MOE_BLOCK
    zero_lo = jnp.concatenate([pad_start + counts, pad_end[-1:]]).astype(jnp.int32)
    zero_hi = jnp.concatenate([pad_end, jnp.full((1,), n_rows, jnp.int32)]).astype(jnp.int32)
    xbuf = _gather(dest, zero_lo, zero_hi, hnp, n_rows)
    ybuf = _moe(block_expert, n_used, xbuf, wgu_bf16, b_gate_up[0][:, None, :],
                wdn_bf16, b_down[0][:, None, :], n_blocks=n_blocks)
    out = _combine(dest, ybuf, h1, route, norm_final_g[None], tc=_pick(t, (128, 64)))
    return out.reshape(batch, seq, d)
```

```python
import functools

import jax
import jax.numpy as jnp
from jax import lax
from jax.experimental import pallas as pl
from jax.experimental.pallas import tpu as pltpu

F32 = jnp.float32
BF16 = jnp.bfloat16

EPS = 1e-6
RG_C = 8.0
TOP_K = 4
SWIGLU_ALPHA = 1.702
SWIGLU_LIMIT = 7.0
MOE_BLOCK = 256
RNN_GROUP_HEADS = 4
CONV_HIST = 32
CONV_STRIDE = 4
RNN_HIST = 8
DMA_UNROLL = 4
LANES = 128
SUBLANES = 8
VMEM_LIMIT = 56 * 1024 * 1024
CAST_BLOCK_ELEMS = 1 << 20


def _params(sem, vmem=VMEM_LIMIT):
    return pltpu.CompilerParams(dimension_semantics=sem, vmem_limit_bytes=vmem)


def _pick(n, cands):
    for c in cands:
        if n % c == 0:
            return c
    return n


def _const_spec(shape):
    zeros = (0,) * len(shape)
    return pl.BlockSpec(shape, lambda *_: zeros)


def _sigmoid(x):
    return 0.5 * jnp.tanh(0.5 * x) + 0.5


def _inproj_body(x_ref, g_ref, w_ref, b_ref, o_ref, xn_ref):
    @pl.when(pl.program_id(1) == 0)
    def _():
        x = x_ref[...]
        ms = jnp.mean(x * x, axis=-1, keepdims=True)
        xn_ref[...] = (x * lax.rsqrt(ms + EPS) * g_ref[...]).astype(BF16)

    acc = jnp.dot(xn_ref[...], w_ref[...], preferred_element_type=F32)
    o_ref[...] = (acc + b_ref[...]).astype(o_ref.dtype)


def _inproj(x2d, g, w, b, tm):
    m, k = x2d.shape
    n = w.shape[1]
    tn = _pick(n, (1024, 512, 256, 128))
    return pl.pallas_call(
        _inproj_body,
        grid=(m // tm, n // tn),
        in_specs=[
            pl.BlockSpec((tm, k), lambda i, j: (i, 0)),
            pl.BlockSpec((1, k), lambda i, j: (0, 0)),
            pl.BlockSpec((k, tn), lambda i, j: (0, j)),
            pl.BlockSpec((1, tn), lambda i, j: (0, j)),
        ],
        out_specs=pl.BlockSpec((tm, tn), lambda i, j: (i, j)),
        out_shape=jax.ShapeDtypeStruct((m, n), BF16),
        scratch_shapes=[pltpu.VMEM((tm, k), BF16)],
        compiler_params=_params(("arbitrary", "arbitrary")),
        name="inproj",
    )(x2d, g, w, b)


def _inproj_cast_body(x_ref, g_ref, w_ref, b_ref, wgu_ref, oc_ref, or_ref, wgu_o_ref, xn_ref, *, nj_c, n_gu):
    i = pl.program_id(0)
    j = pl.program_id(1)

    @pl.when(j == 0)
    def _():
        x = x_ref[...]
        ms = jnp.mean(x * x, axis=-1, keepdims=True)
        xn_ref[...] = (x * lax.rsqrt(ms + EPS) * g_ref[...]).astype(BF16)

    acc = (jnp.dot(xn_ref[...], w_ref[...], preferred_element_type=F32) + b_ref[...]).astype(BF16)

    @pl.when(j < nj_c)
    def _():
        oc_ref[...] = acc

    @pl.when(j >= nj_c)
    def _():
        or_ref[...] = acc

    s = i * pl.num_programs(1) + j

    @pl.when(s < n_gu)
    def _():
        wgu_o_ref[...] = wgu_ref[...].astype(BF16)


def _inproj_cast(x2d, g, w, b, wgu, *, n_c, tm):
    m, k = x2d.shape
    n = w.shape[1]
    tn = _pick(n_c, (512, 256, 128))
    assert (n - n_c) % tn == 0
    nj_c, nj = n_c // tn, n // tn
    n_exp, d, two_ff = wgu.shape
    rows_gu = min(d, max(SUBLANES, CAST_BLOCK_ELEMS // two_ff))
    per_gu = d // rows_gu
    n_gu = n_exp * per_gu
    assert (m // tm) * nj >= n_gu, "not enough grid steps to convert the expert weights"

    def gu_map(i, j):
        blk = jnp.minimum(i * nj + j, n_gu - 1)
        return (blk // per_gu, blk % per_gu, 0)

    body = functools.partial(_inproj_cast_body, nj_c=nj_c, n_gu=n_gu)
    return pl.pallas_call(
        body,
        grid=(m // tm, nj),
        in_specs=[
            pl.BlockSpec((tm, k), lambda i, j: (i, 0)),
            pl.BlockSpec((1, k), lambda i, j: (0, 0)),
            pl.BlockSpec((k, tn), lambda i, j: (0, j)),
            pl.BlockSpec((1, tn), lambda i, j: (0, j)),
            pl.BlockSpec((1, rows_gu, two_ff), gu_map),
        ],
        out_specs=[
            pl.BlockSpec((tm, tn), lambda i, j: (i, jnp.minimum(j, nj_c - 1))),
            pl.BlockSpec((tm, tn), lambda i, j: (i, jnp.maximum(j - nj_c, 0))),
            pl.BlockSpec((1, rows_gu, two_ff), gu_map),
        ],
        out_shape=[
            jax.ShapeDtypeStruct((m, n_c), BF16),
            jax.ShapeDtypeStruct((m, n - n_c), BF16),
            jax.ShapeDtypeStruct(wgu.shape, BF16),
        ],
        scratch_shapes=[pltpu.VMEM((tm, k), BF16)],
        compiler_params=_params(("arbitrary", "arbitrary")),
        name="inproj_cast",
    )(x2d, g, w, b, wgu)


def _conv_body(pc_ref, pcm_ref, dww_ref, dwb_ref, lng_ref, lnb_ref, wpw_ref, bpw_ref, wdn_ref, o_ref, wdn_o_ref,
               ext_ref, cv_ref, yb_ref, *, tm, dc, width, n_meta):
    i = pl.program_id(1)
    nc = dc // LANES
    wdn_o_ref[...] = wdn_ref[...].astype(BF16)

    def cols(c):
        return slice(c * LANES, (c + 1) * LANES)

    @pl.when(i == 0)
    def _():
        for c in range(nc):
            am = pcm_ref[:, cols(c)].astype(F32)
            gm = pcm_ref[:, dc + c * LANES:dc + (c + 1) * LANES].astype(F32)
            ext_ref[c, 0:CONV_HIST - n_meta, :] = jnp.zeros((CONV_HIST - n_meta, LANES), F32)
            ext_ref[c, CONV_HIST - n_meta:CONV_HIST, :] = am * _sigmoid(gm)

    @pl.when(i > 0)
    def _():
        ext_ref[:, 0:CONV_HIST, :] = ext_ref[:, tm:tm + CONV_HIST, :]

    for c in range(nc):
        a = pc_ref[:, cols(c)].astype(F32)
        g = pc_ref[:, dc + c * LANES:dc + (c + 1) * LANES].astype(F32)
        ext_ref[c, CONV_HIST:CONV_HIST + tm, :] = a * _sigmoid(g)

    base = CONV_HIST - (width - 1)
    sub = SUBLANES * CONV_STRIDE

    def col_body(c, carry):
        taps = [jnp.broadcast_to(dww_ref[c, k:k + 1, :], (SUBLANES, LANES)) for k in range(width)]
        for j in range(tm // sub):
            for q in range(CONV_STRIDE):
                acc = None
                for k in range(width):
                    xv = ext_ref[c, pl.ds(j * sub + q + base + k, SUBLANES, stride=CONV_STRIDE), :]
                    acc = taps[k] * xv if acc is None else acc + taps[k] * xv
                cv_ref[c, pl.ds(j * sub + q, SUBLANES, stride=CONV_STRIDE), :] = acc
        return carry

    lax.fori_loop(0, nc, col_body, 0)

    total = None
    for c in range(nc):
        vc = cv_ref[c] + dwb_ref[:, cols(c)]
        cv_ref[c] = vc
        total = vc if total is None else total + vc
    mu = jnp.sum(total, axis=-1, keepdims=True) * (1.0 / dc)
    sq = None
    for c in range(nc):
        dcen = cv_ref[c] - mu
        sq = dcen * dcen if sq is None else sq + dcen * dcen
    rstd = lax.rsqrt(jnp.sum(sq, axis=-1, keepdims=True) * (1.0 / dc) + EPS)
    for c in range(nc):
        y = (cv_ref[c] - mu) * rstd * lng_ref[:, cols(c)] + lnb_ref[:, cols(c)]
        yb_ref[:, cols(c)] = (y * _sigmoid(y)).astype(BF16)
    z = jnp.dot(yb_ref[...], wpw_ref[...], preferred_element_type=F32) + bpw_ref[...]
    gate = _sigmoid(pc_ref[:, 2 * dc:].astype(F32))
    o_ref[...] = gate * z


def _conv_branch(pc, pcm, dww, dwb, lng, lnb, wpw, bpw, wdn, *, batch, seq, tm, d_model):
    width, dc = dww.shape
    nc = dc // LANES
    n_meta = pcm.shape[0]
    nt = seq // tm
    wpad = -(-width // SUBLANES) * SUBLANES
    dww = jnp.pad(dww, ((0, wpad - width), (0, 0))).reshape(wpad, nc, LANES).transpose(1, 0, 2)
    n_exp, d_ff, d_out = wdn.shape
    rows, rem = divmod(n_exp * d_ff, batch * nt)
    assert rem == 0 and (d_ff % rows == 0 or rows % d_ff == 0), "expert weight rows must split evenly over the grid"
    if rows <= d_ff:
        per = d_ff // rows
        wdn_spec = pl.BlockSpec((1, rows, d_out), lambda b, i: ((b * nt + i) // per, (b * nt + i) % per, 0))
    else:
        wdn_spec = pl.BlockSpec((rows // d_ff, d_ff, d_out), lambda b, i: (b * nt + i, 0, 0))
    body = functools.partial(_conv_body, tm=tm, dc=dc, width=width, n_meta=n_meta)
    return pl.pallas_call(
        body,
        grid=(batch, nt),
        in_specs=[
            pl.BlockSpec((tm, pc.shape[1]), lambda b, i: (b * nt + i, 0)),
            _const_spec(pcm.shape),
            _const_spec(dww.shape),
            _const_spec(dwb.shape),
            _const_spec(lng.shape),
            _const_spec(lnb.shape),
            pl.BlockSpec(wpw.shape, lambda b, i: (0, 0), pipeline_mode=pl.Buffered(1)),
            _const_spec(bpw.shape),
            wdn_spec,
        ],
        out_specs=[pl.BlockSpec((tm, d_model), lambda b, i: (b * nt + i, 0)), wdn_spec],
        out_shape=[jax.ShapeDtypeStruct((batch * seq, d_model), F32), jax.ShapeDtypeStruct(wdn.shape, BF16)],
        scratch_shapes=[pltpu.VMEM((nc, tm + CONV_HIST, LANES), F32), pltpu.VMEM((nc, tm, LANES), F32),
                        pltpu.VMEM((tm, dc), BF16)],
        compiler_params=_params(("arbitrary", "arbitrary")),
        name="conv_branch",
    )(pc, pcm, dww, dwb, lng, lnb, wpw, bpw, wdn)


def _gelu_tanh(x):
    return 0.5 * x * (1.0 + jnp.tanh(0.7978845608028654 * (x + 0.044715 * (x * x * x))))


def _softplus(x):
    return jnp.maximum(x, 0.0) + jnp.log(1.0 + jnp.exp(-jnp.abs(x)))


def _rnn_body(pr_ref, prm_ref, mc_ref, cw_ref, cb_ref, wbd_ref, bg_ref, lam_ref, wo_ref, bo_ref, o_ref,
              ext_ref, hc_ref, a_ref, b_ref, h_ref, *, tm, dr, width, n_meta, gw):
    i = pl.program_id(1)
    n_groups = dr // gw
    c_decay = -RG_C * _softplus(-lam_ref[...])
    row = lax.broadcasted_iota(jnp.int32, (SUBLANES, dr), 0)

    def recurrence(n):
        base = RNN_HIST - (width - 1)
        xc = cb_ref[...] + cw_ref[0:1, :] * ext_ref[base:base + n, :]
        for k in range(1, width):
            xc = xc + cw_ref[k:k + 1, :] * ext_ref[base + k:base + k + n, :]
        xcb = xc.astype(BF16)
        for gi in range(n_groups):
            lo, hi = gi * gw, (gi + 1) * gw
            z = jnp.dot(xcb[:, lo:hi], wbd_ref[gi], preferred_element_type=F32) + bg_ref[gi]
            r = _sigmoid(z[:, 0:gw])
            ig = _sigmoid(z[:, gw:2 * gw])
            a = jnp.exp(c_decay[:, lo:hi] * r)
            t = 1.0 - a * a
            mult = jnp.where(t > 0.0, t * lax.rsqrt(t), 0.0)
            a_ref[0:n, lo:hi] = a
            b_ref[0:n, lo:hi] = mult * (ig * xc[:, lo:hi])

        def scan_body(s, carry):
            r0 = pl.multiple_of(s * SUBLANES, SUBLANES)
            av = a_ref[pl.ds(r0, SUBLANES), :]
            bv = b_ref[pl.ds(r0, SUBLANES), :]
            for d in (1, 2, 4):
                keep = row >= d
                a_s = pltpu.roll(av, d, axis=0)
                b_s = pltpu.roll(bv, d, axis=0)
                bv = jnp.where(keep, av * b_s + bv, bv)
                av = jnp.where(keep, av * a_s, av)
            h = av * carry + bv
            h_ref[pl.ds(r0, SUBLANES), :] = h
            return jnp.broadcast_to(h[SUBLANES - 1:SUBLANES, :], (SUBLANES, dr))

        hc_ref[...] = lax.fori_loop(0, n // SUBLANES, scan_body, hc_ref[...])

    @pl.when(i == 0)
    def _():
        ext_ref[0:RNN_HIST, :] = jnp.zeros((RNN_HIST, dr), F32)
        ext_ref[RNN_HIST:RNN_HIST + n_meta, :] = prm_ref[:, 0:dr].astype(F32)
        hc_ref[...] = jnp.zeros((SUBLANES, dr), F32)
        recurrence(n_meta)
        ext_ref[0:RNN_HIST, :] = ext_ref[n_meta:n_meta + RNN_HIST, :]

    @pl.when(i > 0)
    def _():
        ext_ref[0:RNN_HIST, :] = ext_ref[tm:tm + RNN_HIST, :]

    ext_ref[RNN_HIST:RNN_HIST + tm, :] = pr_ref[:, 0:dr].astype(F32)
    recurrence(tm)
    y = h_ref[...] * _gelu_tanh(pr_ref[:, dr:2 * dr].astype(F32))
    z = jnp.dot(y.astype(BF16), wo_ref[...], preferred_element_type=F32) + bo_ref[...]
    gate = _sigmoid(pr_ref[:, 2 * dr:].astype(F32))
    o_ref[...] = mc_ref[...] + gate * z


def _rnn_branch(pr, prm, mc, cw, cb, wbd, bg, lam, wo, bo, *, batch, seq, tm, d_model):
    dr = cw.shape[1]
    width = cw.shape[0]
    n_meta = prm.shape[0]
    gw = wbd.shape[1]
    nt = seq // tm
    body = functools.partial(_rnn_body, tm=tm, dr=dr, width=width, n_meta=n_meta, gw=gw)
    return pl.pallas_call(
        body,
        grid=(batch, nt),
        in_specs=[
            pl.BlockSpec((tm, pr.shape[1]), lambda b, i: (b * nt + i, 0)),
            _const_spec(prm.shape),
            pl.BlockSpec((tm, d_model), lambda b, i: (b * nt + i, 0)),
            _const_spec(cw.shape),
            _const_spec(cb.shape),
            _const_spec(wbd.shape),
            _const_spec(bg.shape),
            _const_spec(lam.shape),
            _const_spec(wo.shape),
            _const_spec(bo.shape),
        ],
        out_specs=pl.BlockSpec((tm, d_model), lambda b, i: (b * nt + i, 0)),
        out_shape=jax.ShapeDtypeStruct((batch * seq, d_model), F32),
        scratch_shapes=[
            pltpu.VMEM((tm + RNN_HIST, dr), F32),
            pltpu.VMEM((SUBLANES, dr), F32),
            pltpu.VMEM((tm, dr), F32),
            pltpu.VMEM((tm, dr), F32),
            pltpu.VMEM((tm, dr), F32),
        ],
        compiler_params=_params(("arbitrary", "arbitrary")),
        name="rnn_branch",
    )(pr, prm, mc, cw, cb, wbd, bg, lam, wo, bo)


def _split_bf16(v):
    hi = v.astype(BF16)
    lo = (v - hi.astype(F32)).astype(BF16)
    return hi, lo


def _post_body(x_ref, mix_ref, wout_ref, bout_ref, gffn_ref, wr_ref, br_ref,
               h1_ref, hnp_ref, route_ref, cnt_ref, run_ref, *, tm, d_model, n_exp):
    step = pl.program_id(0)

    @pl.when(step == 0)
    def _():
        run_ref[...] = jnp.zeros((1, n_exp), F32)

    h1 = x_ref[...] + jnp.dot(mix_ref[...].astype(BF16), wout_ref[...], preferred_element_type=F32) + bout_ref[...]
    h1_ref[...] = h1
    ms = jnp.mean(h1 * h1, axis=-1, keepdims=True)
    hn = h1 * lax.rsqrt(ms + EPS) * gffn_ref[...]

    half = d_model // 2
    hb = hn.astype(BF16).astype(F32)
    lo_bits = lax.shift_right_logical(pltpu.bitcast(hb[:, 0:half], jnp.uint32), jnp.uint32(16))
    hi_bits = pltpu.bitcast(hb[:, half:], jnp.uint32) & jnp.uint32(0xFFFF0000)
    hnp_ref[...] = lo_bits | hi_bits

    hn_hi, hn_lo = _split_bf16(hn)
    w_hi, w_lo = _split_bf16(wr_ref[...])
    logits = (jnp.dot(hn_hi, w_hi, preferred_element_type=F32)
              + jnp.dot(hn_hi, w_lo, preferred_element_type=F32)
              + jnp.dot(hn_lo, w_hi, preferred_element_type=F32)) + br_ref[...]

    lane = lax.broadcasted_iota(jnp.int32, (tm, n_exp), 1).astype(F32)
    work = logits
    vals, idxs = [], []
    for _ in range(TOP_K):
        m = jnp.max(work, axis=-1, keepdims=True)
        ik = jnp.min(jnp.where(work == m, lane, float(n_exp)), axis=-1, keepdims=True)
        vals.append(m)
        idxs.append(ik)
        work = jnp.where(lane == ik, -jnp.inf, work)
    exps = [jnp.exp(v - vals[0]) for v in vals]
    denom = exps[0] + exps[1] + exps[2] + exps[3]

    onehots = [(lane == ik).astype(F32) for ik in idxs]
    sel = onehots[0] + onehots[1] + onehots[2] + onehots[3]
    r_i = lax.broadcasted_iota(jnp.int32, (tm, tm), 0)
    c_i = lax.broadcasted_iota(jnp.int32, (tm, tm), 1)
    strict_lower = (c_i < r_i).astype(BF16)
    before = jnp.dot(strict_lower, sel.astype(BF16), preferred_element_type=F32) + run_ref[...]
    ranks = [jnp.sum(before * oh, axis=-1, keepdims=True) for oh in onehots]
    run_ref[...] = run_ref[...] + jnp.sum(sel, axis=0, keepdims=True)
    cnt_ref[...] = run_ref[...]

    out_lane = lax.broadcasted_iota(jnp.int32, (tm, LANES), 1)
    route = jnp.zeros((tm, LANES), F32)
    for k in range(TOP_K):
        route = jnp.where(out_lane == k, idxs[k], route)
        route = jnp.where(out_lane == TOP_K + k, ranks[k], route)
        route = jnp.where(out_lane == 2 * TOP_K + k, exps[k] / denom, route)
    route_ref[...] = route


def _post(x2d, mixed, wout, bout, gffn, wr, br, *, tm):
    t, d = x2d.shape
    n_exp = wr.shape[1]
    body = functools.partial(_post_body, tm=tm, d_model=d, n_exp=n_exp)
    row_spec = pl.BlockSpec((tm, d), lambda i: (i, 0))
    return pl.pallas_call(
        body,
        grid=(t // tm,),
        in_specs=[row_spec, row_spec, _const_spec(wout.shape), _const_spec(bout.shape), _const_spec(gffn.shape),
                  _const_spec(wr.shape), _const_spec(br.shape)],
        out_specs=[row_spec, pl.BlockSpec((tm, d // 2), lambda i: (i, 0)), pl.BlockSpec((tm, LANES), lambda i: (i, 0)),
                   _const_spec((1, n_exp))],
        out_shape=[jax.ShapeDtypeStruct((t, d), F32), jax.ShapeDtypeStruct((t, d // 2), jnp.uint32),
                   jax.ShapeDtypeStruct((t, LANES), F32), jax.ShapeDtypeStruct((1, n_exp), F32)],
        scratch_shapes=[pltpu.VMEM((1, n_exp), F32)],
        compiler_params=_params(("arbitrary",)),
        name="post_router",
    )(x2d, mixed, wout, bout, gffn, wr, br)


def _gather_body(dest_ref, zlo_ref, zhi_ref, hnp_ref, xbuf_ref, stage_ref, zrow_ref, sems, zsem, *, tg, n_exp):
    s = pl.program_id(0)
    n = pl.num_programs(0)
    slot = s % 2

    def wait_step(sl):
        for _ in range(TOP_K):
            pltpu.make_async_copy(stage_ref.at[sl], xbuf_ref.at[pl.ds(0, tg)], sems.at[sl]).wait()

    @pl.when(s == 0)
    def _():
        zrow_ref[...] = jnp.zeros(zrow_ref.shape, zrow_ref.dtype)

        def pad_copy(r):
            return pltpu.make_async_copy(zrow_ref, xbuf_ref.at[pl.ds(r, 1)], zsem)

        def expert_body(e, carry):
            def start(r, c2):
                pad_copy(r).start()
                return c2

            def wait(r, c2):
                pad_copy(r).wait()
                return c2

            lax.fori_loop(zlo_ref[e], zhi_ref[e], start, 0)
            lax.fori_loop(zlo_ref[e], zhi_ref[e], wait, 0)
            return carry

        lax.fori_loop(0, n_exp, expert_body, 0)

    stage_ref[slot] = hnp_ref[...]

    def issue(j, carry):
        for k in range(TOP_K):
            dst = dest_ref[(s * tg + j) * TOP_K + k]
            pltpu.make_async_copy(stage_ref.at[slot, pl.ds(j, 1)], xbuf_ref.at[pl.ds(dst, 1)], sems.at[slot]).start()
        return carry

    lax.fori_loop(0, tg, issue, 0, unroll=DMA_UNROLL)

    @pl.when(s > 0)
    def _():
        wait_step(1 - slot)

    @pl.when(s == n - 1)
    def _():
        wait_step(slot)


def _gather(dest, zlo, zhi, hnp, n_rows):
    t, width = hnp.shape
    tg = _pick(t, (256, 128, 64))
    body = functools.partial(_gather_body, tg=tg, n_exp=zlo.shape[0])
    return pl.pallas_call(
        body,
        grid_spec=pltpu.PrefetchScalarGridSpec(
            num_scalar_prefetch=3,
            grid=(t // tg,),
            in_specs=[pl.BlockSpec((tg, width), lambda i, *_: (i, 0))],
            out_specs=pl.BlockSpec(memory_space=pl.ANY),
            scratch_shapes=[pltpu.VMEM((2, tg, width), jnp.uint32), pltpu.VMEM((1, width), jnp.uint32),
                            pltpu.SemaphoreType.DMA((2,)), pltpu.SemaphoreType.DMA(())],
        ),
        out_shape=jax.ShapeDtypeStruct((n_rows, width), jnp.uint32),
        compiler_params=_params(("arbitrary",)),
        name="dispatch_gather",
    )(dest, zlo, zhi, hnp)


def _moe_body(be_ref, nu_ref, x_ref, wgu_ref, bgu_ref, wdn_ref, bdn_ref, o_ref, *, d_ff, tf):
    blk = pl.program_id(0)

    @pl.when(blk < nu_ref[0])
    def _():
        w = x_ref[...]
        lo = pltpu.bitcast(lax.shift_left(w, jnp.uint32(16)), F32)
        hi = pltpu.bitcast(w & jnp.uint32(0xFFFF0000), F32)
        xb = jnp.concatenate([lo, hi], axis=1).astype(BF16)
        acc = None
        for j in range(d_ff // tf):
            g = jnp.dot(xb, wgu_ref[0, :, j * tf:(j + 1) * tf], preferred_element_type=F32) + bgu_ref[0, :, j * tf:(j + 1) * tf]
            u = jnp.dot(xb, wgu_ref[0, :, d_ff + j * tf:d_ff + (j + 1) * tf], preferred_element_type=F32) \
                + bgu_ref[0, :, d_ff + j * tf:d_ff + (j + 1) * tf]
            g = jnp.minimum(g, SWIGLU_LIMIT)
            u = jnp.clip(u, -SWIGLU_LIMIT, SWIGLU_LIMIT)
            act = g * _sigmoid(SWIGLU_ALPHA * g) * (u + 1.0)
            part = jnp.dot(act.astype(BF16), wdn_ref[0, j * tf:(j + 1) * tf, :], preferred_element_type=F32)
            acc = part if acc is None else acc + part
        o_ref[...] = acc + bdn_ref[0]

    @pl.when(blk >= nu_ref[0])
    def _():
        o_ref[...] = jnp.zeros(o_ref.shape, o_ref.dtype)


def _moe(block_expert, n_used, xbuf, wgu, bgu, wdn, bdn, *, n_blocks):
    n_exp, d, two_ff = wgu.shape
    d_ff = two_ff // 2
    tf = _pick(d_ff, (512, 256, 128))
    body = functools.partial(_moe_body, d_ff=d_ff, tf=tf)

    def blk_map(i, be, nu):
        return (jnp.minimum(i, nu[0] - 1), 0)

    def exp_map(i, be, nu):
        return (be[jnp.minimum(i, nu[0] - 1)], 0, 0)

    return pl.pallas_call(
        body,
        grid_spec=pltpu.PrefetchScalarGridSpec(
            num_scalar_prefetch=2,
            grid=(n_blocks,),
            in_specs=[
                pl.BlockSpec((MOE_BLOCK, d // 2), blk_map),
                pl.BlockSpec((1, d, two_ff), exp_map),
                pl.BlockSpec((1, 1, two_ff), exp_map),
                pl.BlockSpec((1, d_ff, d), exp_map, pipeline_mode=pl.Buffered(1)),
                pl.BlockSpec((1, 1, d), exp_map),
            ],
            out_specs=pl.BlockSpec((MOE_BLOCK, d), lambda i, be, nu: (i, 0)),
        ),
        out_shape=jax.ShapeDtypeStruct((n_blocks * MOE_BLOCK, d), F32),
        compiler_params=_params(("arbitrary",)),
        name="moe_ffn",
    )(block_expert, n_used, xbuf, wgu, bgu, wdn, bdn)


def _combine_body(dest_ref, y_ref, h1_ref, gates_ref, gfin_ref, o_ref, ybuf_ref, sems, *, tc):
    i = pl.program_id(0)
    n = pl.num_programs(0)

    def issue(step, slot):
        def body(j, carry):
            for k in range(TOP_K):
                d = dest_ref[(step * tc + j) * TOP_K + k]
                pltpu.make_async_copy(y_ref.at[pl.ds(d, 1)], ybuf_ref.at[slot, k, pl.ds(j, 1)], sems.at[slot]).start()
            return carry

        lax.fori_loop(0, tc, body, 0, unroll=DMA_UNROLL)

    @pl.when(i == 0)
    def _():
        issue(0, 0)

    @pl.when(i + 1 < n)
    def _():
        issue(i + 1, (i + 1) % 2)

    slot = i % 2
    for k in range(TOP_K):
        pltpu.make_async_copy(y_ref.at[pl.ds(0, tc)], ybuf_ref.at[slot, k], sems.at[slot]).wait()
    acc = h1_ref[...]
    gates = gates_ref[...]
    for k in range(TOP_K):
        acc = acc + gates[:, 2 * TOP_K + k:2 * TOP_K + k + 1] * ybuf_ref[slot, k]
    ms = jnp.mean(acc * acc, axis=-1, keepdims=True)
    o_ref[...] = acc * lax.rsqrt(ms + EPS) * gfin_ref[...]


def _combine(dest, ybuf, h1, route, gfin, *, tc):
    t, d = h1.shape
    body = functools.partial(_combine_body, tc=tc)
    return pl.pallas_call(
        body,
        grid_spec=pltpu.PrefetchScalarGridSpec(
            num_scalar_prefetch=1,
            grid=(t // tc,),
            in_specs=[
                pl.BlockSpec(memory_space=pl.ANY),
                pl.BlockSpec((tc, d), lambda i, dref: (i, 0)),
                pl.BlockSpec((tc, LANES), lambda i, dref: (i, 0)),
                pl.BlockSpec((1, d), lambda i, dref: (0, 0)),
            ],
            out_specs=pl.BlockSpec((tc, d), lambda i, dref: (i, 0)),
            scratch_shapes=[pltpu.VMEM((2, TOP_K, tc, d), F32), pltpu.SemaphoreType.DMA((2,))],
        ),
        out_shape=jax.ShapeDtypeStruct((t, d), F32),
        compiler_params=_params(("arbitrary",)),
        name="combine_norm",
    )(dest, ybuf, h1, route, gfin)


def _block_diag_gates(wa, ba, wx, bx):
    n_heads, hd, _ = wa.shape
    g = RNN_GROUP_HEADS
    n_groups = n_heads // g
    eye = jnp.eye(g, dtype=wa.dtype)

    def bd(w):
        w = w.reshape(n_groups, g, hd, hd)
        return jnp.einsum("nghj,gk->nghkj", w, eye).reshape(n_groups, g * hd, g * hd)

    wbd = jnp.concatenate([bd(wa), bd(wx)], axis=-1).astype(BF16)
    bg = jnp.concatenate([ba.reshape(n_groups, 1, g * hd), bx.reshape(n_groups, 1, g * hd)], axis=-1)
    return wbd, bg


def kernel(x, meta_tokens, norm_mix_g, w_in, b_in, conv_dw_w, conv_dw_b, conv_ln_g, conv_ln_b, w_conv_out, b_conv_out, rnn_conv_w, rnn_conv_b, rg_a_w, rg_a_b, rg_x_w, rg_x_b, rg_lambda, w_rnn_out, b_rnn_out, w_out, b_out, norm_ffn_g, w_router, b_router, w_gate_up, b_gate_up, w_down, b_down, norm_final_g):
    batch, seq, d = x.shape
    assert norm_mix_g.shape[0] == 1, "single-layer block"
    dc = conv_dw_w.shape[2]
    dr = rnn_conv_w.shape[2]
    n_exp = w_router.shape[2]
    t = batch * seq
    o1, o2, o3 = 2 * dc, 2 * dc + dr, 2 * dc + 2 * dr

    w_in0, b_in0 = w_in[0], b_in[0]
    n_c = o1 + d
    w_all = jnp.concatenate([w_in0[:, :o1], w_in0[:, o3:o3 + d], w_in0[:, o1:o3], w_in0[:, o3 + d:]], axis=1).astype(BF16)
    b_all = jnp.concatenate([b_in0[:o1], b_in0[o3:o3 + d], b_in0[o1:o3], b_in0[o3 + d:]])[None]
    wbd, bg = _block_diag_gates(rg_a_w[0], rg_a_b[0], rg_x_w[0], rg_x_b[0])
    g_mix = norm_mix_g[0][None]

    x2d = x.reshape(t, d)
    tm_in = _pick(t, (1024, 512, 256, 128))
    tm = _pick(seq, (256, 128))

    pc, pr, wgu_bf16 = _inproj_cast(x2d, g_mix, w_all, b_all, w_gate_up[0], n_c=n_c, tm=tm_in)
    pm = _inproj(meta_tokens, g_mix, w_all, b_all, meta_tokens.shape[0])
    pcm, prm = pm[:, :n_c], pm[:, n_c:]

    mc, wdn_bf16 = _conv_branch(pc, pcm, conv_dw_w[0], conv_dw_b[0][None], conv_ln_g[0][None], conv_ln_b[0][None],
                                w_conv_out[0].astype(BF16), b_conv_out[0][None], w_down[0],
                                batch=batch, seq=seq, tm=tm, d_model=d)
    mixed = _rnn_branch(pr, prm, mc, rnn_conv_w[0], rnn_conv_b[0][None], wbd, bg, rg_lambda[0][None],
                        w_rnn_out[0].astype(BF16), b_rnn_out[0][None], batch=batch, seq=seq, tm=tm, d_model=d)
    wr = jnp.pad(w_router[0], ((0, 0), (0, LANES - n_exp)))
    br = jnp.pad(b_router[0], (0, LANES - n_exp), constant_values=-jnp.inf)[None]
    h1, hnp, route, counts = _post(x2d, mixed, w_out[0].astype(BF16), b_out[0][None], norm_ffn_g[0][None],
                                   wr, br, tm=tm)

    counts = counts[0, :n_exp].astype(jnp.int32)
    padded = (counts + MOE_BLOCK - 1) // MOE_BLOCK * MOE_BLOCK
    pad_end = jnp.cumsum(padded)
    pad_start = pad_end - padded
    n_blocks = (t * TOP_K + n_exp * (MOE_BLOCK - 1)) // MOE_BLOCK
    n_used = (pad_end[-1] // MOE_BLOCK).astype(jnp.int32)[None]
    block_expert = jnp.minimum(
        jnp.sum((jnp.arange(n_blocks, dtype=jnp.int32)[:, None] * MOE_BLOCK >= pad_end[None, :]).astype(jnp.int32), axis=1),
        n_exp - 1).astype(jnp.int32)
    idx = route[:, 0:TOP_K].astype(jnp.int32)
    rank = route[:, TOP_K:2 * TOP_K].astype(jnp.int32)
    start_of = jnp.sum(jnp.where(idx[..., None] == jnp.arange(n_exp, dtype=jnp.int32), pad_start, 0), axis=-1)
    dest = (start_of + rank).reshape(t * TOP_K)

    n_rows = n_blocks * MOE_BLOCK
    zero_lo = jnp.concatenate([pad_start + counts, pad_end[-1:]]).astype(jnp.int32)
    zero_hi = jnp.concatenate([pad_end, jnp.full((1,), n_rows, jnp.int32)]).astype(jnp.int32)
    xbuf = _gather(dest, zero_lo, zero_hi, hnp, n_rows)
    ybuf = _moe(block_expert, n_used, xbuf, wgu_bf16, b_gate_up[0][:, None, :],
                wdn_bf16, b_down[0][:, None, :], n_blocks=n_blocks)
    out = _combine(dest, ybuf, h1, route, norm_final_g[None], tc=_pick(t, (128, 64)))
    return out.reshape(batch, seq, d)
```
